```python
import math
import jax, jax.numpy as jnp
from jax import lax
import numpy as np

D_MODEL = 2048
BATCH = 2
SEQ = 4096
DEPTH = 2
DEC_BATCH = 4
DEC_SEQ = 2048
PAST_LEN = 128

A_HEADS = 8
A_HEAD_DIM = 128
A_WIDTH = A_HEADS * A_HEAD_DIM
A_CONV = 3
GDN_CHUNK = 64
B_WIDTH = 1024
B_CONV = 3
C_GROUPS = ((128, 1), (512, 4), (2048, 16))
C_HEADS_PER_GROUP = 4
C_HEADS = C_HEADS_PER_GROUP * 3
C_HEAD_DIM = 128
C_WIDTH = C_HEADS * C_HEAD_DIM
C_OUT = C_HEADS_PER_GROUP * C_HEAD_DIM
N_BUCKETS = 32
REL_MAX_DIST = 2048
N_BRANCHES = 3
D_FF = 4 * D_MODEL
EPS = 1e-6
NEG_INF = -1e30
SPLIT_SIZES = (3 * A_WIDTH, A_WIDTH, 2 * A_HEADS, 2 * A_HEADS, B_WIDTH, B_WIDTH, B_WIDTH, C_WIDTH, C_WIDTH, C_WIDTH, N_BRANCHES * D_MODEL)
IN_COLS = sum(SPLIT_SIZES)

kernel_name = 'hybrid_bidir_gdn_shortconv_dilated_encoder'


def rms_norm(x, w):
    x32 = x.astype(jnp.float32)
    y = x32 * lax.rsqrt(jnp.mean(x32 * x32, axis=-1, keepdims=True) + EPS)
    return (y * w.astype(jnp.float32)).astype(x.dtype)


def l2_normalize(x):
    return x * lax.rsqrt(jnp.sum(x * x, axis=-1, keepdims=True) + EPS)


def dwconv_centred(x, w):
    k = w.shape[0]
    r = k // 2
    s = x.shape[1]
    xp = jnp.pad(x, ((0, 0), (r, r), (0, 0)))
    y = xp[:, 0:s] * w[0]
    for j in range(1, k):
        y = y + xp[:, j:j + s] * w[j]
    return y


def gated_delta_rule_chunked(q, k, v, g, beta):
    b, h, s, dk = q.shape
    dv = v.shape[-1]
    c = GDN_CHUNK
    n = s // c
    q, k, v = (t.reshape(b, h, n, c, t.shape[-1]) for t in (q, k, v))
    gc = jnp.cumsum(g.reshape(b, h, n, c), axis=-1)
    beta = beta.reshape(b, h, n, c, 1)
    incl = jnp.tril(jnp.ones((c, c), dtype=bool))
    strict = jnp.tril(jnp.ones((c, c), dtype=bool), -1)
    decay = jnp.exp(jnp.where(incl, gc[..., :, None] - gc[..., None, :], -jnp.inf))
    kb = k * beta
    vb = v * beta
    lower = jnp.where(strict, jnp.einsum('bhncd,bhnsd->bhncs', kb, k) * decay, 0.0)
    t_mat = jnp.eye(c, dtype=q.dtype) + lower
    u = lax.linalg.triangular_solve(t_mat, vb, left_side=True, lower=True, unit_diagonal=True)
    w = lax.linalg.triangular_solve(t_mat, kb * jnp.exp(gc)[..., None], left_side=True, lower=True, unit_diagonal=True)
    intra = jnp.where(incl, jnp.einsum('bhncd,bhnsd->bhncs', q, k) * decay, 0.0)

    def step(state, xs):
        q_i, k_i, u_i, w_i, gc_i, a_i = xs
        v_new = u_i - jnp.einsum('bhcd,bhde->bhce', w_i, state)
        o_i = (jnp.einsum('bhcd,bhde->bhce', q_i * jnp.exp(gc_i)[..., None], state)
               + jnp.einsum('bhcs,bhse->bhce', a_i, v_new))
        g_last = gc_i[..., -1:]
        k_dec = k_i * jnp.exp(g_last - gc_i)[..., None]
        state = state * jnp.exp(g_last)[..., None] + jnp.einsum('bhcd,bhce->bhde', k_dec, v_new)
        return state, o_i

    xs = tuple(jnp.moveaxis(t, 2, 0) for t in (q, k, u, w, gc, intra))
    state0 = jnp.zeros((b, h, dk, dv), q.dtype)
    _, o = lax.scan(step, state0, xs)
    return jnp.moveaxis(o, 0, 2).reshape(b, h, s, dv)


def gdn_mixer(qkv, z, a, bt, a_log, dt_bias, head_norm):
    dtype = z.dtype
    bsz, s, _ = qkv.shape
    f32 = jnp.float32
    qkv = jax.nn.silu(qkv.astype(f32))
    q, k, v = (t.reshape(bsz, s, A_HEADS, A_HEAD_DIM).transpose(0, 2, 1, 3) for t in jnp.split(qkv, 3, axis=-1))
    q = l2_normalize(q) * (A_HEAD_DIM ** -0.5)
    k = l2_normalize(k)
    a = a.astype(f32).reshape(bsz, s, 2, A_HEADS)
    g = -jnp.exp(a_log.astype(f32)) * jax.nn.softplus(a + dt_bias.astype(f32))
    beta = jax.nn.sigmoid(bt.astype(f32).reshape(bsz, s, 2, A_HEADS))
    g = g.transpose(2, 0, 3, 1)
    beta = beta.transpose(2, 0, 3, 1)
    o_fwd = gated_delta_rule_chunked(q, k, v, g[0], beta[0])
    rev = lambda t: jnp.flip(t, axis=2)
    o_bwd = rev(gated_delta_rule_chunked(rev(q), rev(k), rev(v), rev(g[1]), rev(beta[1])))
    o = (o_fwd + o_bwd).transpose(0, 2, 1, 3)
    o = o * lax.rsqrt(jnp.mean(o * o, axis=-1, keepdims=True) + EPS) * head_norm.astype(f32)
    o = o * jax.nn.silu(z.astype(f32).reshape(bsz, s, A_HEADS, A_HEAD_DIM))
    return o.reshape(bsz, s, A_WIDTH).astype(dtype)


def t5_bucket(rel):
    half = N_BUCKETS // 2
    exact = half // 2
    ret = np.where(rel > 0, half, 0)
    n = np.abs(rel)
    large = exact + (np.log(np.maximum(n, 1) / exact) / math.log(REL_MAX_DIST / exact) * (half - exact)).astype(np.int32)
    large = np.minimum(large, half - 1)
    return (ret + np.where(n < exact, n, large)).astype(np.int32)


def banded_attention(q, k, v, bias_tab, radius, dil):
    n, l, h, dh = q.shape
    blk = radius
    nb = -(-l // blk)
    pad = nb * blk - l
    qb = jnp.pad(q, ((0, 0), (0, pad), (0, 0), (0, 0))).reshape(n, nb, blk, h, dh)

    def key_blocks(t):
        tp = jnp.pad(t, ((0, 0), (blk, pad + blk), (0, 0), (0, 0)))
        return jnp.concatenate([tp[:, j * blk:j * blk + nb * blk].reshape(n, nb, blk, h, dh) for j in range(3)], axis=2)

    kb = key_blocks(k)
    vb = key_blocks(v)
    rel0 = np.arange(3 * blk)[None, :] - blk - np.arange(blk)[:, None]
    bias = bias_tab[t5_bucket(rel0 * dil)].transpose(2, 0, 1)
    kpos = (np.arange(nb)[:, None, None] - 1) * blk + np.arange(3 * blk)[None, None, :]
    valid = (np.abs(rel0)[None] <= radius) & (kpos >= 0) & (kpos < l)
    sc = jnp.einsum('nbqhd,nbkhd->nbhqk', qb, kb) * (dh ** -0.5) + bias[None, None]
    sc = jnp.where(valid[None, :, None], sc, NEG_INF)
    m = jnp.max(sc, axis=-1, keepdims=True)
    p = jnp.exp(sc - m)
    den = jnp.sum(p, axis=-1, keepdims=True)
    o = jnp.einsum('nbhqk,nbkhd->nbqhd', p / den, vb).reshape(n, nb * blk, h, dh)[:, :l]
    lse = (m + jnp.log(den))[..., 0].transpose(0, 1, 3, 2).reshape(n, nb * blk, h)[:, :l]
    return o, lse


def dilated_mixer(q, k, v, rel_bias):
    dtype = q.dtype
    bsz, s, _ = q.shape
    shp = (bsz, s, C_HEADS, C_HEAD_DIM)
    q, k, v = (t.astype(jnp.float32).reshape(shp) for t in (q, k, v))
    outs, lses = [], []
    for gi, (window, dil) in enumerate(C_GROUPS):
        hs = slice(gi * C_HEADS_PER_GROUP, (gi + 1) * C_HEADS_PER_GROUP)
        l = s // dil

        def to_sub(t):
            return t[:, :, hs].reshape(bsz, l, dil, C_HEADS_PER_GROUP, C_HEAD_DIM).transpose(0, 2, 1, 3, 4).reshape(bsz * dil, l, C_HEADS_PER_GROUP, C_HEAD_DIM)

        o, lse = banded_attention(to_sub(q), to_sub(k), to_sub(v), rel_bias[:, hs].astype(jnp.float32), window // (2 * dil), dil)
        outs.append(o.reshape(bsz, dil, l, C_HEADS_PER_GROUP, C_HEAD_DIM).transpose(0, 2, 1, 3, 4).reshape(bsz, s, C_HEADS_PER_GROUP, C_HEAD_DIM))
        lses.append(lse.reshape(bsz, dil, l, C_HEADS_PER_GROUP).transpose(0, 2, 1, 3).reshape(bsz, s, C_HEADS_PER_GROUP))
    alpha = jax.nn.softmax(jnp.stack(lses), axis=0)
    o = jnp.einsum('gbsh,gbshd->bshd', alpha, jnp.stack(outs))
    return o.reshape(bsz, s, C_OUT).astype(dtype)


def trunk(x, rel_bias, norm_mix, w_in, conv_a, a_log, dt_bias, head_norm, conv_b,
          w_br_a, w_br_b, w_br_c, w_out, norm_mlp, w_up, w_down, norm_final):
    split_points = np.cumsum(SPLIT_SIZES)[:-1].tolist()
    h = x
    for layer in range(DEPTH):
        xn = rms_norm(h, norm_mix[layer])
        proj = xn @ w_in[layer]
        (a_qkv, a_z, a_alpha, a_beta, b_gb, b_gc, b_h, c_q, c_k, c_v, gates) = jnp.split(proj, split_points, axis=-1)
        y_a = gdn_mixer(dwconv_centred(a_qkv, conv_a[layer]), a_z, a_alpha, a_beta,
                        a_log[layer], dt_bias[layer], head_norm[layer])
        y_b = b_gb * dwconv_centred(b_gc * b_h, conv_b[layer])
        y_c = dilated_mixer(c_q, c_k, c_v, rel_bias)
        g_a, g_b, g_c = jnp.split(jax.nn.sigmoid(gates), N_BRANCHES, axis=-1)
        merged = g_a * (y_a @ w_br_a[layer]) + g_b * (y_b @ w_br_b[layer]) + g_c * (y_c @ w_br_c[layer])
        h = h + merged @ w_out[layer]
        xn = rms_norm(h, norm_mlp[layer])
        h = h + jnp.square(jax.nn.relu(xn @ w_up[layer])) @ w_down[layer]
    return rms_norm(h, norm_final)


def setup_inputs(seed: int = 0) -> dict:
    key = jax.random.key(seed)
    ks = jax.random.split(key, 20)
    f32 = jnp.float32

    def normal(k, shape, scale):
        return jax.random.normal(k, shape, f32) * scale

    def gain(k, shape):
        return 1.0 + 0.02 * jax.random.normal(k, shape, f32)

    dt = jnp.exp(jax.random.uniform(ks[6], (DEPTH, 2, A_HEADS), f32, math.log(1e-3), math.log(1e-1)))
    return {
        'x_prompt': jax.random.normal(ks[0], (BATCH, SEQ, D_MODEL), f32),
        'x_sample': jax.random.normal(ks[1], (DEC_BATCH, DEC_SEQ, D_MODEL), f32),
        'rel_bias': normal(ks[2], (N_BUCKETS, C_HEADS), 0.2),
        'norm_mix': gain(ks[3], (DEPTH, D_MODEL)),
        'w_in': normal(ks[4], (DEPTH, D_MODEL, IN_COLS), D_MODEL ** -0.5),
        'conv_a': normal(ks[5], (DEPTH, A_CONV, 3 * A_WIDTH), A_CONV ** -0.5),
        'a_log': jnp.log(jax.random.uniform(ks[7], (DEPTH, 2, A_HEADS), f32, 1.0, 16.0)),
        'dt_bias': dt + jnp.log(-jnp.expm1(-dt)),
        'head_norm': gain(ks[8], (DEPTH, A_HEAD_DIM)),
        'conv_b': normal(ks[9], (DEPTH, B_CONV, B_WIDTH), B_CONV ** -0.5),
        'w_br_a': normal(ks[10], (DEPTH, A_WIDTH, D_MODEL), A_WIDTH ** -0.5),
        'w_br_b': normal(ks[11], (DEPTH, B_WIDTH, D_MODEL), B_WIDTH ** -0.5),
        'w_br_c': normal(ks[12], (DEPTH, C_OUT, D_MODEL), C_OUT ** -0.5),
        'w_out': normal(ks[13], (DEPTH, D_MODEL, D_MODEL), D_MODEL ** -0.5),
        'norm_mlp': gain(ks[14], (DEPTH, D_MODEL)),
        'w_up': normal(ks[15], (DEPTH, D_MODEL, D_FF), D_MODEL ** -0.5),
        'w_down': normal(ks[16], (DEPTH, D_FF, D_MODEL), D_FF ** -0.5),
        'norm_final': gain(ks[17], (D_MODEL,)),
    }


def reference(x_prompt, x_sample, rel_bias, norm_mix, w_in, conv_a, a_log, dt_bias, head_norm, conv_b,
              w_br_a, w_br_b, w_br_c, w_out, norm_mlp, w_up, w_down, norm_final):
    y_prompt = trunk(x_prompt, rel_bias, norm_mix, w_in, conv_a, a_log, dt_bias, head_norm, conv_b,
                     w_br_a, w_br_b, w_br_c, w_out, norm_mlp, w_up, w_down, norm_final)
    y_sample = trunk(x_sample, rel_bias, norm_mix, w_in, conv_a, a_log, dt_bias, head_norm, conv_b,
                     w_br_a, w_br_b, w_br_c, w_out, norm_mlp, w_up, w_down, norm_final)
    return (y_prompt, y_sample)
```

```python
import functools
import math

import jax
import jax.numpy as jnp
import numpy as np
from jax import lax
from jax.experimental import pallas as pl
from jax.experimental.pallas import tpu as pltpu

F32 = jnp.float32
BF16 = jnp.bfloat16

LANE = 128
VMEM_BYTES_V7X = 64 * 1024 * 1024

D_MODEL = 2048
A_HEADS = 8
HEAD_DIM = 128
A_WIDTH = A_HEADS * HEAD_DIM
GDN_CHUNK = 64
B_WIDTH = 1024
C_GROUPS = ((128, 1), (512, 4), (2048, 16))
C_HPG = 4
C_HEADS = 12
C_WIDTH = C_HEADS * HEAD_DIM
C_OUT = C_HPG * HEAD_DIM
N_BUCKETS = 32
REL_MAX_DIST = 2048
RADIUS = 64
D_FF = 4 * D_MODEL
EPS = 1e-6
NEG_INF = -1e30

P_QKV = 0
P_B3 = 3072
P_Z = 6144
P_AB = 7168
P_CQKV = 7680
P_GATES = 12288
P_WIDTH = 18432
Y_WIDTH = A_WIDTH + B_WIDTH + C_OUT


def _vmem_limit(nbytes):
    return int(min(max(nbytes, 16 * 1024 * 1024), VMEM_BYTES_V7X - 8 * 1024 * 1024))


def _seq_bounds(t0, geom):
    tp, sp, ss = geom
    in_p = t0 < tp
    length = jnp.where(in_p, sp, ss)
    base = jnp.where(in_p, 0, tp)
    lo = base + ((t0 - base) // length) * length
    return lo, lo + length


def _rms(x, w):
    ms = jnp.mean(x * x, axis=-1, keepdims=True)
    return x * lax.rsqrt(ms + EPS) * w


def _sigmoid(x):
    return 1.0 / (1.0 + jnp.exp(-x))


def _dot(a, b):
    return jnp.dot(a, b, preferred_element_type=F32)


def _dot_nt(a, b):
    return lax.dot_general(a, b, (((1,), (1,)), ((), ())), preferred_element_type=F32)


def _dot_tn(a, b):
    return lax.dot_general(a, b, (((0,), (0,)), ((), ())), preferred_element_type=F32)


def _inproj_kernel(x_ref, nw_ref, w_ref, p_ref, ab_ref, xn_ref, *, ab_tile, ab_off):
    j = pl.program_id(1)

    @pl.when(j == 0)
    def _():
        xn_ref[...] = _rms(x_ref[...], nw_ref[...]).astype(BF16)

    acc = _dot(xn_ref[...], w_ref[...])
    p_ref[...] = acc.astype(BF16)

    @pl.when(j == ab_tile)
    def _():
        ab_ref[...] = acc[:, ab_off:ab_off + LANE]


def _in_proj(h, norm_w, w_packed, *, tm=1024, tn=1024):
    t = h.shape[0]
    tm = min(tm, t)
    assert t % tm == 0 and P_WIDTH % tn == 0
    kern = functools.partial(_inproj_kernel, ab_tile=P_AB // tn, ab_off=P_AB % tn)
    vmem = 2 * tm * D_MODEL * 4 + tm * D_MODEL * 2 + 2 * D_MODEL * tn * 2 + 2 * tm * tn * 2 + 2 * tm * tn * 4
    return pl.pallas_call(
        kern,
        out_shape=(jax.ShapeDtypeStruct((t, P_WIDTH), BF16), jax.ShapeDtypeStruct((t, LANE), F32)),
        grid=(t // tm, P_WIDTH // tn),
        in_specs=[
            pl.BlockSpec((tm, D_MODEL), lambda i, j: (i, 0)),
            pl.BlockSpec((1, D_MODEL), lambda i, j: (0, 0)),
            pl.BlockSpec((D_MODEL, tn), lambda i, j: (0, j)),
        ],
        out_specs=(
            pl.BlockSpec((tm, tn), lambda i, j: (i, j)),
            pl.BlockSpec((tm, LANE), lambda i, j: (i, 0)),
        ),
        scratch_shapes=[pltpu.VMEM((tm, D_MODEL), BF16)],
        compiler_params=pltpu.CompilerParams(
            dimension_semantics=("parallel", "arbitrary"), vmem_limit_bytes=_vmem_limit(vmem + (8 << 20))),
        name="in_proj",
    )(h, norm_w.reshape(1, D_MODEL), w_packed)


def _conv3(x, prev_row, next_row, w_ref):
    tm = x.shape[0]
    rows = lax.broadcasted_iota(jnp.int32, (tm, 1), 0)
    x_m1 = jnp.where(rows == 0, prev_row, pltpu.roll(x, 1, 0))
    x_p1 = jnp.where(rows == tm - 1, next_row, pltpu.roll(x, tm - 1, 0))
    return x_m1 * w_ref[0:1, :] + x * w_ref[1:2, :] + x_p1 * w_ref[2:3, :]


def _halo_flags(tm, geom):
    t0 = pl.program_id(0) * tm
    lo, hi = _seq_bounds(t0, geom)
    return (t0 > lo).astype(F32), (t0 + tm < hi).astype(F32)


def _gdn_pre_kernel(x_ref, xp_ref, xn_ref, cw_ref, ab_ref, nalog_ref, dtb_ref, o_ref, gb_ref, *, tm, geom, halo):
    has_prev, has_next = _halo_flags(tm, geom)
    x = x_ref[...].astype(F32)
    prev_row = xp_ref[halo - 1:halo, :].astype(F32) * has_prev
    next_row = xn_ref[0:1, :].astype(F32) * has_next
    y = _conv3(x, prev_row, next_row, cw_ref)
    y = y * _sigmoid(y)
    for hd in range(2 * A_HEADS):
        sl = slice(hd * HEAD_DIM, (hd + 1) * HEAD_DIM)
        yh = y[:, sl]
        inv = lax.rsqrt(jnp.sum(yh * yh, axis=-1, keepdims=True) + EPS)
        if hd < A_HEADS:
            inv = inv * (HEAD_DIM ** -0.5)
        o_ref[:, sl] = (yh * inv).astype(BF16)
    o_ref[:, 2 * A_WIDTH:] = y[:, 2 * A_WIDTH:].astype(BF16)

    ab = ab_ref[...]
    xs = ab + dtb_ref[...]
    softplus = jnp.maximum(xs, 0.0) + jnp.log(1.0 + jnp.exp(-jnp.abs(xs)))
    g = nalog_ref[...] * softplus
    lane = lax.broadcasted_iota(jnp.int32, ab.shape, 1)
    gb_ref[...] = jnp.where(lane < 2 * A_HEADS, g, _sigmoid(ab))


def _gdn_pre(p, ab, conv_w, a_log, dt_bias, geom, *, tm=256, halo=16):
    t = p.shape[0]
    tm = min(tm, t)
    w = 3 * A_WIDTH
    nb = tm // halo
    last = t // halo - 1
    nalog = jnp.zeros((1, LANE), F32).at[0, :2 * A_HEADS].set(-jnp.exp(a_log.reshape(-1)))
    dtb = jnp.zeros((1, LANE), F32).at[0, :2 * A_HEADS].set(dt_bias.reshape(-1))
    kern = functools.partial(_gdn_pre_kernel, tm=tm, geom=geom, halo=halo)
    vmem = 2 * (tm * w * 2 * 2 + 2 * halo * w * 2) + 10 * tm * w * 4
    return pl.pallas_call(
        kern,
        out_shape=(jax.ShapeDtypeStruct((t, w), BF16), jax.ShapeDtypeStruct((t, LANE), F32)),
        grid=(t // tm,),
        in_specs=[
            pl.BlockSpec((tm, w), lambda i: (i, P_QKV // w)),
            pl.BlockSpec((halo, w), lambda i: (jnp.maximum(i * nb - 1, 0), P_QKV // w)),
            pl.BlockSpec((halo, w), lambda i: (jnp.minimum((i + 1) * nb, last), P_QKV // w)),
            pl.BlockSpec((3, w), lambda i: (0, 0)),
            pl.BlockSpec((tm, LANE), lambda i: (i, 0)),
            pl.BlockSpec((1, LANE), lambda i: (0, 0)),
            pl.BlockSpec((1, LANE), lambda i: (0, 0)),
        ],
        out_specs=(
            pl.BlockSpec((tm, w), lambda i: (i, 0)),
            pl.BlockSpec((tm, LANE), lambda i: (i, 0)),
        ),
        compiler_params=pltpu.CompilerParams(
            dimension_semantics=("parallel",), vmem_limit_bytes=_vmem_limit(vmem)),
        name="gdn_pre",
    )(p, p, p, conv_w, ab, nalog, dtb)


def _split3(x):
    x1 = x.astype(BF16)
    r1 = x - x1.astype(F32)
    x2 = r1.astype(BF16)
    r2 = r1 - x2.astype(F32)
    return x1, x2, r2.astype(BF16)


def _unit_tri_inverse(l_mat, bd16, off32, off64, eye):
    def mm(a, b):
        return _dot(a.astype(BF16), b.astype(BF16))

    ld = jnp.where(bd16, l_mat, 0.0)
    l2 = mm(ld, ld)
    l4 = mm(l2, l2)
    l8 = mm(l4, l4)
    x = eye - ld
    x = x + mm(x, l2)
    x = x + mm(x, l4)
    x = x + mm(x, l8)
    e32 = jnp.where(off32, l_mat, 0.0)
    x = x - mm(mm(x, e32), x)
    e64 = jnp.where(off64, l_mat, 0.0)
    x = x - mm(mm(x, e64), x)
    return x


def _gdn_scan_kernel(qf_ref, kf_ref, vf_ref, gf_ref, grf_ref, qb_ref, kb_ref, vb_ref, gbk_ref, grb_ref,
                     of_ref, ob_ref, s_ref, *, rb, nblk, geom):
    c = GDN_CHUNK
    nc = rb // c
    i = pl.program_id(0)
    t0f = i * rb
    t0b = (nblk - 1 - i) * rb
    lo_f, _ = _seq_bounds(t0f, geom)
    _, hi_b = _seq_bounds(t0b, geom)

    @pl.when(t0f == lo_f)
    def _():
        s_ref[0:A_HEADS] = jnp.zeros((A_HEADS, HEAD_DIM, HEAD_DIM), F32)

    @pl.when(t0b + rb == hi_b)
    def _():
        s_ref[A_HEADS:2 * A_HEADS] = jnp.zeros((A_HEADS, HEAD_DIM, HEAD_DIM), F32)

    ri = lax.broadcasted_iota(jnp.int32, (c, c), 0)
    ci = lax.broadcasted_iota(jnp.int32, (c, c), 1)
    lower_incl = ri >= ci
    upper_incl = ri <= ci
    tri_lo = jnp.where(lower_incl, 1.0, 0.0).astype(BF16)
    tri_up = jnp.where(upper_incl, 1.0, 0.0).astype(BF16)
    bd16 = (ri // 16) == (ci // 16)
    off32 = ((ri // 32) == (ci // 32)) & jnp.logical_not(bd16)
    off64 = (ri // 32) != (ci // 32)
    eye = jnp.where(ri == ci, 1.0, 0.0).astype(F32)

    def chunk(direction, cidx, q_ref, k_ref, v_ref, g_ref, gr_ref, o_ref):
        r0 = pl.multiple_of(cidx * c, c)
        gcol = g_ref[pl.ds(r0, c), :]
        grow = gr_ref[cidx]
        if direction == 0:
            m_col, m_row, incl, strict = tri_lo, tri_up, lower_incl, ri > ci
        else:
            m_col, m_row, incl, strict = tri_up, tri_lo, upper_incl, ri < ci
        g1, g2, g3 = _split3(gcol)
        gc_col = _dot(m_col, g1) + _dot(m_col, g2) + _dot(m_col, g3)
        h1, h2, h3 = _split3(grow)
        gc_row = _dot(h1, m_row) + _dot(h2, m_row) + _dot(h3, m_row)
        for hh in range(A_HEADS):
            ch = direction * A_HEADS + hh
            sl = slice(hh * HEAD_DIM, (hh + 1) * HEAD_DIM)
            qh = q_ref[pl.ds(r0, c), sl]
            kh = k_ref[pl.ds(r0, c), sl]
            vh = v_ref[pl.ds(r0, c), sl]
            gcc = gc_col[:, ch:ch + 1]
            gcr = gc_row[ch:ch + 1, :]
            beta = gcol[:, 2 * A_HEADS + ch:2 * A_HEADS + ch + 1]
            decay = jnp.exp(jnp.where(incl, gcc - gcr, NEG_INF))
            qk_kk = _dot_nt(jnp.concatenate([qh, kh], axis=0), kh)
            intra = qk_kk[:c] * decay
            l_mat = jnp.where(strict, qk_kk[c:] * beta * decay, 0.0)
            t_inv = _unit_tri_inverse(l_mat, bd16, off32, off64, eye)
            eg = jnp.exp(gcc)
            kf32 = kh.astype(F32)
            rhs = jnp.concatenate(
                [(vh.astype(F32) * beta).astype(BF16), (kf32 * (beta * eg)).astype(BF16)], axis=1)
            uw = _dot(t_inv.astype(BF16), rhs)
            state = s_ref[ch]
            wq = jnp.concatenate([uw[:, HEAD_DIM:].astype(BF16), (qh.astype(F32) * eg).astype(BF16)], axis=0)
            ws_qs = _dot(wq, state.astype(BF16))
            v_new = (uw[:, :HEAD_DIM] - ws_qs[:c]).astype(BF16)
            o = ws_qs[c:] + _dot(intra.astype(BF16), v_new)
            o_ref[pl.ds(r0, c), sl] = o.astype(o_ref.dtype)
            g_last = gcc[c - 1:c, :] if direction == 0 else gcc[0:1, :]
            k_dec = (kf32 * jnp.exp(g_last - gcc)).astype(BF16)
            s_ref[ch] = state * jnp.exp(g_last) + _dot_tn(k_dec, v_new)

    def body(j, carry):
        chunk(0, j, qf_ref, kf_ref, vf_ref, gf_ref, grf_ref, of_ref)
        chunk(1, nc - 1 - j, qb_ref, kb_ref, vb_ref, gbk_ref, grb_ref, ob_ref)
        return carry

    lax.fori_loop(0, nc, body, 0)


def _gdn_scan(qkv, gb, gbt3, geom, *, rb=512):
    t = qkv.shape[0]
    rb = min(rb, t)
    nblk = t // rb
    ncb = rb // GDN_CHUNK
    kern = functools.partial(_gdn_scan_kernel, rb=rb, nblk=nblk, geom=geom)

    def fwd(col):
        return lambda i: (i, col)

    def bwd(col):
        return lambda i: (nblk - 1 - i, col)

    in_specs = []
    for mk in (fwd, bwd):
        in_specs += [
            pl.BlockSpec((rb, A_WIDTH), mk(0)),
            pl.BlockSpec((rb, A_WIDTH), mk(1)),
            pl.BlockSpec((rb, A_WIDTH), mk(2)),
            pl.BlockSpec((rb, LANE), mk(0)),
            pl.BlockSpec((ncb, 32, GDN_CHUNK), (lambda i: (i, 0, 0)) if mk is fwd else (lambda i: (nblk - 1 - i, 0, 0))),
        ]
    vmem = 2 * (6 * rb * A_WIDTH * 2 + 2 * rb * LANE * 4 + 2 * rb * A_WIDTH * 2) + 16 * HEAD_DIM * HEAD_DIM * 4
    return pl.pallas_call(
        kern,
        out_shape=(jax.ShapeDtypeStruct((t, A_WIDTH), BF16), jax.ShapeDtypeStruct((t, A_WIDTH), BF16)),
        grid=(nblk,),
        in_specs=in_specs,
        out_specs=(pl.BlockSpec((rb, A_WIDTH), fwd(0)), pl.BlockSpec((rb, A_WIDTH), bwd(0))),
        scratch_shapes=[pltpu.VMEM((2 * A_HEADS, HEAD_DIM, HEAD_DIM), F32)],
        compiler_params=pltpu.CompilerParams(
            dimension_semantics=("arbitrary",), vmem_limit_bytes=_vmem_limit(vmem + (16 << 20))),
        name="gdn_scan",
    )(qkv, qkv, qkv, gb, gbt3, qkv, qkv, qkv, gb, gbt3)


def _t5_bucket(rel):
    half = N_BUCKETS // 2
    exact = half // 2
    ret = np.where(rel > 0, half, 0)
    n = np.abs(rel)
    large = exact + (np.log(np.maximum(n, 1) / exact) / math.log(REL_MAX_DIST / exact) * (half - exact)).astype(np.int32)
    large = np.minimum(large, half - 1)
    return (ret + np.where(n < exact, n, large)).astype(np.int32)


def _attn_kernel(tab_ref, bkt_ref, q_ref, kp_ref, kc_ref, kn_ref, vp_ref, vc_ref, vn_ref,
                 o_ref, lse_ref, bias_ref, *, tq, dil, geom):
    r = pl.program_id(0)
    qb = pl.program_id(1)
    nk = tq + 2 * RADIUS

    @pl.when((r == 0) & (qb == 0))
    def _():
        bkt = bkt_ref[...]
        for hh in range(C_HPG):
            acc = jnp.full((tq, nk), NEG_INF, F32)
            for b in range(N_BUCKETS):
                acc = jnp.where(bkt == b, tab_ref[b * C_HPG + hh], acc)
            bias_ref[hh] = acc

    lo, hi = _seq_bounds(qb * tq * dil, geom)
    krow = qb * tq - RADIUS + lax.broadcasted_iota(jnp.int32, (1, nk), 1)
    in_seq = (krow * dil >= lo) & (krow * dil < hi)
    scale = HEAD_DIM ** -0.5
    for hh in range(C_HPG):
        sl = slice(hh * HEAD_DIM, (hh + 1) * HEAD_DIM)
        kwin = jnp.concatenate([kp_ref[tq - RADIUS:, sl], kc_ref[:, sl], kn_ref[:RADIUS, sl]], axis=0)
        vwin = jnp.concatenate([vp_ref[tq - RADIUS:, sl], vc_ref[:, sl], vn_ref[:RADIUS, sl]], axis=0)
        sc = _dot_nt(q_ref[:, sl], kwin) * scale + bias_ref[hh]
        sc = jnp.where(in_seq, sc, NEG_INF)
        m = jnp.max(sc, axis=-1, keepdims=True)
        p = jnp.exp(sc - m)
        den = jnp.sum(p, axis=-1, keepdims=True)
        o = _dot(p.astype(BF16), vwin) / den
        o_ref[:, sl] = o.astype(o_ref.dtype)
        lse_ref[:, sl] = jnp.broadcast_to(m + jnp.log(den), (tq, HEAD_DIM))


def _attn_group(p, rel_bias, gi, geom, *, tq=128):
    t = p.shape[0]
    dil = C_GROUPS[gi][1]
    assert C_GROUPS[gi][0] // (2 * dil) == RADIUS
    rows = t // dil
    tq = min(tq, min(geom[1], geom[2]) // dil)
    assert tq >= RADIUS and rows % tq == 0 and (geom[1] // dil) % tq == 0 and (geom[2] // dil) % tq == 0
    nq = rows // tq
    nk = tq + 2 * RADIUS
    rel = np.arange(nk)[None, :] - RADIUS - np.arange(tq)[:, None]
    bkt = np.where(np.abs(rel) <= RADIUS, _t5_bucket(rel * dil), -1).astype(np.int32)
    tab = rel_bias[:, gi * C_HPG:(gi + 1) * C_HPG].astype(F32).reshape(-1)
    pv = p.reshape(rows, dil * P_WIDTH)
    cw = C_OUT
    pblk = P_WIDTH // cw
    qcol = P_CQKV // cw + gi
    kcol = qcol + C_WIDTH // cw
    vcol = kcol + C_WIDTH // cw

    def cur(col):
        return lambda r, qb: (qb, r * pblk + col)

    def prev(col):
        return lambda r, qb: (jnp.maximum(qb - 1, 0), r * pblk + col)

    def nxt(col):
        return lambda r, qb: (jnp.minimum(qb + 1, nq - 1), r * pblk + col)

    blk = (tq, cw)
    kern = functools.partial(_attn_kernel, tq=tq, dil=dil, geom=geom)
    o, lse = pl.pallas_call(
        kern,
        out_shape=(jax.ShapeDtypeStruct((rows, dil * cw), BF16), jax.ShapeDtypeStruct((rows, dil * cw), F32)),
        grid=(dil, nq),
        in_specs=[
            pl.BlockSpec(memory_space=pltpu.SMEM),
            pl.BlockSpec((tq, nk), lambda r, qb: (0, 0)),
            pl.BlockSpec(blk, cur(qcol)),
            pl.BlockSpec(blk, prev(kcol)), pl.BlockSpec(blk, cur(kcol)), pl.BlockSpec(blk, nxt(kcol)),
            pl.BlockSpec(blk, prev(vcol)), pl.BlockSpec(blk, cur(vcol)), pl.BlockSpec(blk, nxt(vcol)),
        ],
        out_specs=(pl.BlockSpec(blk, lambda r, qb: (qb, r)), pl.BlockSpec(blk, lambda r, qb: (qb, r))),
        scratch_shapes=[pltpu.VMEM((C_HPG, tq, nk), F32)],
        compiler_params=pltpu.CompilerParams(dimension_semantics=("arbitrary", "arbitrary")),
        name=f"attn_g{gi}",
    )(tab, jnp.asarray(bkt), pv, pv, pv, pv, pv, pv, pv)
    return o.reshape(t, cw), lse.reshape(t, cw)


def _mix_prep_kernel(of_ref, ob_ref, z_ref, hn_ref, b_ref, bp_ref, bn_ref, cw_ref,
                     o0_ref, o1_ref, o2_ref, l0_ref, l1_ref, l2_ref, y_ref, *, tm, geom, halo):
    o = of_ref[...].astype(F32) + ob_ref[...].astype(F32)
    z = z_ref[...].astype(F32)
    for hd in range(A_HEADS):
        sl = slice(hd * HEAD_DIM, (hd + 1) * HEAD_DIM)
        oh = o[:, sl]
        zh = z[:, sl]
        y = _rms(oh, hn_ref[...]) * (zh * _sigmoid(zh))
        y_ref[:, sl] = y.astype(BF16)

    has_prev, has_next = _halo_flags(tm, geom)

    def gated(ref, rows):
        return ref[rows, B_WIDTH:2 * B_WIDTH].astype(F32) * ref[rows, 2 * B_WIDTH:].astype(F32)

    u = gated(b_ref, slice(None))
    prev_row = gated(bp_ref, slice(halo - 1, halo)) * has_prev
    next_row = gated(bn_ref, slice(0, 1)) * has_next
    yb = b_ref[:, :B_WIDTH].astype(F32) * _conv3(u, prev_row, next_row, cw_ref)
    y_ref[:, A_WIDTH:A_WIDTH + B_WIDTH] = yb.astype(BF16)

    l0, l1, l2 = l0_ref[...], l1_ref[...], l2_ref[...]
    m = jnp.maximum(jnp.maximum(l0, l1), l2)
    e0, e1, e2 = jnp.exp(l0 - m), jnp.exp(l1 - m), jnp.exp(l2 - m)
    num = e0 * o0_ref[...].astype(F32) + e1 * o1_ref[...].astype(F32) + e2 * o2_ref[...].astype(F32)
    y_ref[:, A_WIDTH + B_WIDTH:] = (num / (e0 + e1 + e2)).astype(BF16)


def _mix_prep(p, o_f, o_b, attn, head_norm, conv_b, geom, *, tm=256, halo=16):
    t = p.shape[0]
    tm = min(tm, t)
    nb = tm // halo
    last = t // halo - 1
    w3 = 3 * B_WIDTH
    row = lambda w, col=0: pl.BlockSpec((tm, w), lambda i: (i, col))
    kern = functools.partial(_mix_prep_kernel, tm=tm, geom=geom, halo=halo)
    (o0, l0), (o1, l1), (o2, l2) = attn
    vmem = 2 * tm * (2 * A_WIDTH * 2 + A_WIDTH * 2 + w3 * 2 + 3 * C_OUT * 6 + Y_WIDTH * 2) + 12 * tm * A_WIDTH * 4
    return pl.pallas_call(
        kern,
        out_shape=jax.ShapeDtypeStruct((t, Y_WIDTH), BF16),
        grid=(t // tm,),
        in_specs=[
            row(A_WIDTH), row(A_WIDTH), row(A_WIDTH, P_Z // A_WIDTH),
            pl.BlockSpec((1, HEAD_DIM), lambda i: (0, 0)),
            row(w3, P_B3 // w3),
            pl.BlockSpec((halo, w3), lambda i: (jnp.maximum(i * nb - 1, 0), P_B3 // w3)),
            pl.BlockSpec((halo, w3), lambda i: (jnp.minimum((i + 1) * nb, last), P_B3 // w3)),
            pl.BlockSpec((3, B_WIDTH), lambda i: (0, 0)),
            row(C_OUT), row(C_OUT), row(C_OUT), row(C_OUT), row(C_OUT), row(C_OUT),
        ],
        out_specs=row(Y_WIDTH),
        compiler_params=pltpu.CompilerParams(
            dimension_semantics=("parallel",), vmem_limit_bytes=_vmem_limit(vmem)),
        name="mix_prep",
    )(o_f, o_b, p, head_norm.reshape(1, HEAD_DIM), p, p, p, conv_b, o0, o1, o2, l0, l1, l2)


def _merge_kernel(y_ref, g_ref, h_ref, wa_ref, wb_ref, wc_ref, wo_ref, o_ref):
    d = D_MODEL
    merged = _sigmoid(g_ref[:, 0:d].astype(F32)) * _dot(y_ref[:, :A_WIDTH], wa_ref[...])
    merged += _sigmoid(g_ref[:, d:2 * d].astype(F32)) * _dot(y_ref[:, A_WIDTH:A_WIDTH + B_WIDTH], wb_ref[...])
    merged += _sigmoid(g_ref[:, 2 * d:].astype(F32)) * _dot(y_ref[:, A_WIDTH + B_WIDTH:], wc_ref[...])
    o_ref[...] = h_ref[...] + _dot(merged.astype(BF16), wo_ref[...])


def _merge(y, p, h, wa, wb, wc, wo, *, tm=256):
    t = h.shape[0]
    tm = min(tm, t)
    d = D_MODEL
    const = lambda shape: pl.BlockSpec(shape, lambda i: (0, 0), pipeline_mode=pl.Buffered(1))
    wbytes = (A_WIDTH + B_WIDTH + C_OUT + d) * d * 2
    vmem = wbytes + 2 * tm * (Y_WIDTH * 2 + 3 * d * 2 + d * 4 + d * 4) + 6 * tm * d * 4
    return pl.pallas_call(
        _merge_kernel,
        out_shape=jax.ShapeDtypeStruct((t, d), F32),
        grid=(t // tm,),
        in_specs=[
            pl.BlockSpec((tm, Y_WIDTH), lambda i: (i, 0)),
            pl.BlockSpec((tm, 3 * d), lambda i: (i, P_GATES // (3 * d))),
            pl.BlockSpec((tm, d), lambda i: (i, 0)),
            const((A_WIDTH, d)), const((B_WIDTH, d)), const((C_OUT, d)), const((d, d)),
        ],
        out_specs=pl.BlockSpec((tm, d), lambda i: (i, 0)),
        compiler_params=pltpu.CompilerParams(
            dimension_semantics=("parallel",), vmem_limit_bytes=_vmem_limit(vmem)),
        name="merge",
    )(y, p, h, wa, wb, wc, wo)


def _mlp_kernel(h_ref, nw_ref, wu_ref, wd_ref, fw_ref, o_ref, xn_ref, acc_ref, *, final):
    f = pl.program_id(1)

    @pl.when(f == 0)
    def _():
        xn_ref[...] = _rms(h_ref[...], nw_ref[...]).astype(BF16)
        acc_ref[...] = jnp.zeros_like(acc_ref)

    up = jnp.maximum(_dot(xn_ref[...], wu_ref[...]), 0.0)
    acc_ref[...] += _dot((up * up).astype(BF16), wd_ref[...])

    @pl.when(f == pl.num_programs(1) - 1)
    def _():
        hn = h_ref[...] + acc_ref[...]
        o_ref[...] = _rms(hn, fw_ref[...]) if final else hn


def _mlp(h, norm_w, wu, wd, final_w, *, final, tm=512, tf=1024):
    t = h.shape[0]
    tm = min(tm, t)
    d = D_MODEL
    kern = functools.partial(_mlp_kernel, final=final)
    vmem = 4 * tm * d * 4 + tm * d * 2 + tm * d * 4 + 4 * d * tf * 2 + 3 * tm * tf * 4
    return pl.pallas_call(
        kern,
        out_shape=jax.ShapeDtypeStruct((t, d), F32),
        grid=(t // tm, D_FF // tf),
        in_specs=[
            pl.BlockSpec((tm, d), lambda i, f: (i, 0)),
            pl.BlockSpec((1, d), lambda i, f: (0, 0)),
            pl.BlockSpec((d, tf), lambda i, f: (0, f)),
            pl.BlockSpec((tf, d), lambda i, f: (f, 0)),
            pl.BlockSpec((1, d), lambda i, f: (0, 0)),
        ],
        out_specs=pl.BlockSpec((tm, d), lambda i, f: (i, 0)),
        scratch_shapes=[pltpu.VMEM((tm, d), BF16), pltpu.VMEM((tm, d), F32)],
        compiler_params=pltpu.CompilerParams(
            dimension_semantics=("parallel", "arbitrary"), vmem_limit_bytes=_vmem_limit(vmem + (8 << 20))),
        name="mlp",
    )(h, norm_w.reshape(1, d), wu, wd, final_w.reshape(1, d))


def _pack_w_in(w):
    d = w.shape[0]
    a_qkv, a_z = w[:, :3072], w[:, 3072:4096]
    ab = w[:, 4096:4128]
    b3 = w[:, 4128:7200]
    cqkv = w[:, 7200:11808]
    gates = w[:, 11808:]
    zeros = lambda n: jnp.zeros((d, n), w.dtype)
    packed = jnp.concatenate(
        [a_qkv, b3, a_z, ab, zeros(P_CQKV - P_AB - 32), cqkv, gates], axis=1)
    assert packed.shape[1] == P_WIDTH
    return packed.astype(BF16)


def _trunk(h, geom, rel_bias, norm_mix, w_in, conv_a, a_log, dt_bias, head_norm, conv_b,
           w_br_a, w_br_b, w_br_c, w_out, norm_mlp, w_up, w_down, norm_final):
    depth = w_in.shape[0]
    t = h.shape[0]
    for layer in range(depth):
        p, ab = _in_proj(h, norm_mix[layer], _pack_w_in(w_in[layer]))
        qkv, gb = _gdn_pre(p, ab, conv_a[layer], a_log[layer], dt_bias[layer], geom)
        gbt3 = gb[:, :32].reshape(t // GDN_CHUNK, GDN_CHUNK, 32).transpose(0, 2, 1)
        o_f, o_b = _gdn_scan(qkv, gb, gbt3, geom)
        attn = [_attn_group(p, rel_bias, gi, geom) for gi in range(len(C_GROUPS))]
        y = _mix_prep(p, o_f, o_b, attn, head_norm[layer], conv_b[layer], geom)
        h = _merge(y, p, h, w_br_a[layer].astype(BF16), w_br_b[layer].astype(BF16),
                   w_br_c[layer].astype(BF16), w_out[layer].astype(BF16))
        h = _mlp(h, norm_mlp[layer], w_up[layer].astype(BF16), w_down[layer].astype(BF16),
                 norm_final, final=(layer == depth - 1))
    return h


def kernel(x_prompt, x_sample, rel_bias, norm_mix, w_in, conv_a, a_log, dt_bias, head_norm, conv_b,
           w_br_a, w_br_b, w_br_c, w_out, norm_mlp, w_up, w_down, norm_final):
    bp, sp, d = x_prompt.shape
    bs, ss, _ = x_sample.shape
    tp = bp * sp
    geom = (tp, sp, ss)
    h = jnp.concatenate([x_prompt.reshape(tp, d), x_sample.reshape(bs * ss, d)], axis=0)
    y = _trunk(h, geom, rel_bias, norm_mix, w_in, conv_a, a_log, dt_bias, head_norm, conv_b,
               w_br_a, w_br_b, w_br_c, w_out, norm_mlp, w_up, w_down, norm_final)
    return y[:tp].reshape(bp, sp, d), y[tp:].reshape(bs, ss, d)
```

```python
import functools
import math

import jax
import jax.numpy as jnp
import numpy as np
from jax import lax
from jax.experimental import pallas as pl
from jax.experimental.pallas import tpu as pltpu

F32 = jnp.float32
BF16 = jnp.bfloat16

LANE = 128
VMEM_BYTES_V7X = 64 * 1024 * 1024

D_MODEL = 2048
A_HEADS = 8
HEAD_DIM = 128
A_WIDTH = A_HEADS * HEAD_DIM
GDN_CHUNK = 64
B_WIDTH = 1024
C_GROUPS = ((128, 1), (512, 4), (2048, 16))
C_HPG = 4
C_HEADS = 12
C_WIDTH = C_HEADS * HEAD_DIM
C_OUT = C_HPG * HEAD_DIM
N_BUCKETS = 32
REL_MAX_DIST = 2048
RADIUS = 64
D_FF = 4 * D_MODEL
EPS = 1e-6
NEG_INF = -1e30

P_QKV = 0
P_B3 = 3072
P_Z = 6144
P_AB = 7168
P_CQKV = 7680
P_GATES = 12288
P_WIDTH = 18432
Y_WIDTH = A_WIDTH + B_WIDTH + C_OUT


def _vmem_limit(nbytes):
    return int(min(max(nbytes, 16 * 1024 * 1024), VMEM_BYTES_V7X - 8 * 1024 * 1024))


def _seq_bounds(t0, geom):
    tp, sp, ss = geom
    in_p = t0 < tp
    length = jnp.where(in_p, sp, ss)
    base = jnp.where(in_p, 0, tp)
    lo = base + ((t0 - base) // length) * length
    return lo, lo + length


def _rms(x, w):
    ms = jnp.mean(x * x, axis=-1, keepdims=True)
    return x * lax.rsqrt(ms + EPS) * w


def _sigmoid(x):
    return 1.0 / (1.0 + jnp.exp(-x))


def _dot(a, b):
    return jnp.dot(a, b, preferred_element_type=F32)


def _dot_nt(a, b):
    return lax.dot_general(a, b, (((1,), (1,)), ((), ())), preferred_element_type=F32)


def _dot_tn(a, b):
    return lax.dot_general(a, b, (((0,), (0,)), ((), ())), preferred_element_type=F32)


def _inproj_kernel(x_ref, nw_ref, w_ref, p_ref, ab_ref, xn_ref, *, ab_tile, ab_off):
    j = pl.program_id(1)

    @pl.when(j == 0)
    def _():
        xn_ref[...] = _rms(x_ref[...], nw_ref[...]).astype(BF16)

    acc = _dot(xn_ref[...], w_ref[...])
    p_ref[...] = acc.astype(BF16)

    @pl.when(j == ab_tile)
    def _():
        ab_ref[...] = acc[:, ab_off:ab_off + LANE]


def _in_proj(h, norm_w, w_packed, *, tm=1024, tn=1024):
    t = h.shape[0]
    tm = min(tm, t)
    assert t % tm == 0 and P_WIDTH % tn == 0
    kern = functools.partial(_inproj_kernel, ab_tile=P_AB // tn, ab_off=P_AB % tn)
    vmem = 2 * tm * D_MODEL * 4 + tm * D_MODEL * 2 + 2 * D_MODEL * tn * 2 + 2 * tm * tn * 2 + 2 * tm * tn * 4
    return pl.pallas_call(
        kern,
        out_shape=(jax.ShapeDtypeStruct((t, P_WIDTH), BF16), jax.ShapeDtypeStruct((t, LANE), F32)),
        grid=(t // tm, P_WIDTH // tn),
        in_specs=[
            pl.BlockSpec((tm, D_MODEL), lambda i, j: (i, 0)),
            pl.BlockSpec((1, D_MODEL), lambda i, j: (0, 0)),
            pl.BlockSpec((D_MODEL, tn), lambda i, j: (0, j)),
        ],
        out_specs=(
            pl.BlockSpec((tm, tn), lambda i, j: (i, j)),
            pl.BlockSpec((tm, LANE), lambda i, j: (i, 0)),
        ),
        scratch_shapes=[pltpu.VMEM((tm, D_MODEL), BF16)],
        compiler_params=pltpu.CompilerParams(
            dimension_semantics=("parallel", "arbitrary"), vmem_limit_bytes=_vmem_limit(vmem + (8 << 20))),
        name="in_proj",
    )(h, norm_w.reshape(1, D_MODEL), w_packed)


def _conv3(x, prev_row, next_row, w_ref):
    tm = x.shape[0]
    rows = lax.broadcasted_iota(jnp.int32, (tm, 1), 0)
    x_m1 = jnp.where(rows == 0, prev_row, pltpu.roll(x, 1, 0))
    x_p1 = jnp.where(rows == tm - 1, next_row, pltpu.roll(x, tm - 1, 0))
    return x_m1 * w_ref[0:1, :] + x * w_ref[1:2, :] + x_p1 * w_ref[2:3, :]


def _halo_flags(tm, geom):
    t0 = pl.program_id(0) * tm
    lo, hi = _seq_bounds(t0, geom)
    return (t0 > lo).astype(F32), (t0 + tm < hi).astype(F32)


def _gdn_pre_kernel(x_ref, xp_ref, xn_ref, cw_ref, ab_ref, nalog_ref, dtb_ref, o_ref, gb_ref, *, tm, geom, halo):
    has_prev, has_next = _halo_flags(tm, geom)
    x = x_ref[...].astype(F32)
    prev_row = xp_ref[halo - 1:halo, :].astype(F32) * has_prev
    next_row = xn_ref[0:1, :].astype(F32) * has_next
    y = _conv3(x, prev_row, next_row, cw_ref)
    y = y * _sigmoid(y)
    for hd in range(2 * A_HEADS):
        sl = slice(hd * HEAD_DIM, (hd + 1) * HEAD_DIM)
        yh = y[:, sl]
        inv = lax.rsqrt(jnp.sum(yh * yh, axis=-1, keepdims=True) + EPS)
        if hd < A_HEADS:
            inv = inv * (HEAD_DIM ** -0.5)
        o_ref[:, sl] = (yh * inv).astype(BF16)
    o_ref[:, 2 * A_WIDTH:] = y[:, 2 * A_WIDTH:].astype(BF16)

    ab = ab_ref[...]
    xs = ab + dtb_ref[...]
    softplus = jnp.maximum(xs, 0.0) + jnp.log(1.0 + jnp.exp(-jnp.abs(xs)))
    g = nalog_ref[...] * softplus
    lane = lax.broadcasted_iota(jnp.int32, ab.shape, 1)
    gb_ref[...] = jnp.where(lane < 2 * A_HEADS, g, _sigmoid(ab))


def _gdn_pre(p, ab, conv_w, a_log, dt_bias, geom, *, tm=256, halo=16):
    t = p.shape[0]
    tm = min(tm, t)
    w = 3 * A_WIDTH
    nb = tm // halo
    last = t // halo - 1
    nalog = jnp.zeros((1, LANE), F32).at[0, :2 * A_HEADS].set(-jnp.exp(a_log.reshape(-1)))
    dtb = jnp.zeros((1, LANE), F32).at[0, :2 * A_HEADS].set(dt_bias.reshape(-1))
    kern = functools.partial(_gdn_pre_kernel, tm=tm, geom=geom, halo=halo)
    vmem = 2 * (tm * w * 2 * 2 + 2 * halo * w * 2) + 10 * tm * w * 4
    return pl.pallas_call(
        kern,
        out_shape=(jax.ShapeDtypeStruct((t, w), BF16), jax.ShapeDtypeStruct((t, LANE), F32)),
        grid=(t // tm,),
        in_specs=[
            pl.BlockSpec((tm, w), lambda i: (i, P_QKV // w)),
            pl.BlockSpec((halo, w), lambda i: (jnp.maximum(i * nb - 1, 0), P_QKV // w)),
            pl.BlockSpec((halo, w), lambda i: (jnp.minimum((i + 1) * nb, last), P_QKV // w)),
            pl.BlockSpec((3, w), lambda i: (0, 0)),
            pl.BlockSpec((tm, LANE), lambda i: (i, 0)),
            pl.BlockSpec((1, LANE), lambda i: (0, 0)),
            pl.BlockSpec((1, LANE), lambda i: (0, 0)),
        ],
        out_specs=(
            pl.BlockSpec((tm, w), lambda i: (i, 0)),
            pl.BlockSpec((tm, LANE), lambda i: (i, 0)),
        ),
        compiler_params=pltpu.CompilerParams(
            dimension_semantics=("parallel",), vmem_limit_bytes=_vmem_limit(vmem)),
        name="gdn_pre",
    )(p, p, p, conv_w, ab, nalog, dtb)


def _split3(x):
    x1 = x.astype(BF16)
    r1 = x - x1.astype(F32)
    x2 = r1.astype(BF16)
    r2 = r1 - x2.astype(F32)
    return x1, x2, r2.astype(BF16)


def _gdn_scan_kernel(qf_ref, kf_ref, vf_ref, gf_ref, grf_ref, qb_ref, kb_ref, vb_ref, gbk_ref, grb_ref,
                     of_ref, ob_ref, s_ref, *, rb, nblk, geom):
    c = GDN_CHUNK
    nc = rb // c
    i = pl.program_id(0)
    t0f = i * rb
    t0b = (nblk - 1 - i) * rb
    lo_f, _ = _seq_bounds(t0f, geom)
    _, hi_b = _seq_bounds(t0b, geom)

    @pl.when(t0f == lo_f)
    def _():
        s_ref[0:A_HEADS] = jnp.zeros((A_HEADS, HEAD_DIM, HEAD_DIM), F32)

    @pl.when(t0b + rb == hi_b)
    def _():
        s_ref[A_HEADS:2 * A_HEADS] = jnp.zeros((A_HEADS, HEAD_DIM, HEAD_DIM), F32)

    ri = lax.broadcasted_iota(jnp.int32, (c, c), 0)
    ci = lax.broadcasted_iota(jnp.int32, (c, c), 1)
    lower_incl = ri >= ci
    upper_incl = ri <= ci
    tri_lo = jnp.where(lower_incl, 1.0, 0.0).astype(BF16)
    tri_up = jnp.where(upper_incl, 1.0, 0.0).astype(BF16)
    bd16 = (ri // 16) == (ci // 16)
    off32 = ((ri // 32) == (ci // 32)) & jnp.logical_not(bd16)
    off64 = (ri // 32) != (ci // 32)
    eye = jnp.where(ri == ci, 1.0, 0.0).astype(F32)
    bf = lambda x: x.astype(BF16)

    def body(j, carry):
        sides = (
            (0, j, qf_ref, kf_ref, vf_ref, gf_ref, grf_ref, of_ref, tri_lo, tri_up, lower_incl, ri > ci),
            (1, nc - 1 - j, qb_ref, kb_ref, vb_ref, gbk_ref, grb_ref, ob_ref, tri_up, tri_lo, upper_incl, ri < ci),
        )
        chains = []
        for direction, cidx, q_ref, k_ref, v_ref, g_ref, gr_ref, o_ref, m_col, m_row, incl, strict in sides:
            r0 = pl.multiple_of(cidx * c, c)
            gcol = g_ref[pl.ds(r0, c), :]
            grow = gr_ref[cidx]
            g1, g2, g3 = _split3(gcol)
            gc_col = _dot(m_col, g1) + _dot(m_col, g2) + _dot(m_col, g3)
            h1, h2, h3 = _split3(grow)
            gc_row = _dot(h1, m_row) + _dot(h2, m_row) + _dot(h3, m_row)
            for hh in range(A_HEADS):
                ch = direction * A_HEADS + hh
                sl = slice(hh * HEAD_DIM, (hh + 1) * HEAD_DIM)
                gcc = gc_col[:, ch:ch + 1]
                chains.append(dict(
                    ch=ch, sl=sl, r0=r0, o_ref=o_ref, incl=incl, strict=strict,
                    q=q_ref[pl.ds(r0, c), sl], k=k_ref[pl.ds(r0, c), sl], v=v_ref[pl.ds(r0, c), sl],
                    gcc=gcc, gcr=gc_row[ch:ch + 1, :],
                    beta=gcol[:, 2 * A_HEADS + ch:2 * A_HEADS + ch + 1],
                    g_last=gcc[c - 1:c, :] if direction == 0 else gcc[0:1, :]))

        gram = [_dot_nt(jnp.concatenate([x["q"], x["k"]], axis=0), x["k"]) for x in chains]
        ld, e32, e64, intra = [], [], [], []
        for x, gm in zip(chains, gram):
            decay = jnp.exp(jnp.where(x["incl"], x["gcc"] - x["gcr"], NEG_INF))
            intra.append(bf(gm[:c] * decay))
            l_mat = jnp.where(x["strict"], gm[c:] * x["beta"] * decay, 0.0)
            ld.append(jnp.where(bd16, l_mat, 0.0))
            e32.append(bf(jnp.where(off32, l_mat, 0.0)))
            e64.append(bf(jnp.where(off64, l_mat, 0.0)))

        ldb = [bf(a) for a in ld]
        l2 = [bf(_dot(a, a)) for a in ldb]
        xs = [eye - a for a in ld]
        l4 = [bf(_dot(a, a)) for a in l2]
        xs = [x + _dot(bf(x), a) for x, a in zip(xs, l2)]
        l8 = [bf(_dot(a, a)) for a in l4]
        xs = [x + _dot(bf(x), a) for x, a in zip(xs, l4)]
        xs = [x + _dot(bf(x), a) for x, a in zip(xs, l8)]
        for e in (e32, e64):
            xb = [bf(x) for x in xs]
            ts = [bf(_dot(b, a)) for b, a in zip(xb, e)]
            xs = [x - _dot(t, b) for x, t, b in zip(xs, ts, xb)]

        uw, qg, k_dec = [], [], []
        for x, t_inv in zip(chains, xs):
            eg = jnp.exp(x["gcc"])
            kf32 = x["k"].astype(F32)
            rhs = jnp.concatenate(
                [bf(x["v"].astype(F32) * x["beta"]), bf(kf32 * (x["beta"] * eg))], axis=1)
            uw.append(_dot(bf(t_inv), rhs))
            qg.append(bf(x["q"].astype(F32) * eg))
            k_dec.append(bf(kf32 * jnp.exp(x["g_last"] - x["gcc"])))

        states = [s_ref[x["ch"]] for x in chains]
        ws_qs = [_dot(jnp.concatenate([bf(a[:, HEAD_DIM:]), b], axis=0), bf(s))
                 for a, b, s in zip(uw, qg, states)]
        v_new = [bf(a[:, :HEAD_DIM] - r[:c]) for a, r in zip(uw, ws_qs)]
        for x, r, a, vn in zip(chains, ws_qs, intra, v_new):
            o = r[c:] + _dot(a, vn)
            x["o_ref"][pl.ds(x["r0"], c), x["sl"]] = o.astype(x["o_ref"].dtype)
        for x, s, kd, vn in zip(chains, states, k_dec, v_new):
            s_ref[x["ch"]] = s * jnp.exp(x["g_last"]) + _dot_tn(kd, vn)
        return carry

    lax.fori_loop(0, nc, body, 0)


def _gdn_scan(qkv, gb, gbt3, geom, *, rb=512):
    t = qkv.shape[0]
    rb = min(rb, t)
    nblk = t // rb
    ncb = rb // GDN_CHUNK
    kern = functools.partial(_gdn_scan_kernel, rb=rb, nblk=nblk, geom=geom)

    def fwd(col):
        return lambda i: (i, col)

    def bwd(col):
        return lambda i: (nblk - 1 - i, col)

    in_specs = []
    for mk in (fwd, bwd):
        in_specs += [
            pl.BlockSpec((rb, A_WIDTH), mk(0)),
            pl.BlockSpec((rb, A_WIDTH), mk(1)),
            pl.BlockSpec((rb, A_WIDTH), mk(2)),
            pl.BlockSpec((rb, LANE), mk(0)),
            pl.BlockSpec((ncb, 32, GDN_CHUNK), (lambda i: (i, 0, 0)) if mk is fwd else (lambda i: (nblk - 1 - i, 0, 0))),
        ]
    vmem = 2 * (6 * rb * A_WIDTH * 2 + 2 * rb * LANE * 4 + 2 * rb * A_WIDTH * 2) + 16 * HEAD_DIM * HEAD_DIM * 4
    return pl.pallas_call(
        kern,
        out_shape=(jax.ShapeDtypeStruct((t, A_WIDTH), BF16), jax.ShapeDtypeStruct((t, A_WIDTH), BF16)),
        grid=(nblk,),
        in_specs=in_specs,
        out_specs=(pl.BlockSpec((rb, A_WIDTH), fwd(0)), pl.BlockSpec((rb, A_WIDTH), bwd(0))),
        scratch_shapes=[pltpu.VMEM((2 * A_HEADS, HEAD_DIM, HEAD_DIM), F32)],
        compiler_params=pltpu.CompilerParams(
            dimension_semantics=("arbitrary",), vmem_limit_bytes=_vmem_limit(vmem + (16 << 20))),
        name="gdn_scan",
    )(qkv, qkv, qkv, gb, gbt3, qkv, qkv, qkv, gb, gbt3)


def _t5_bucket(rel):
    half = N_BUCKETS // 2
    exact = half // 2
    ret = np.where(rel > 0, half, 0)
    n = np.abs(rel)
    large = exact + (np.log(np.maximum(n, 1) / exact) / math.log(REL_MAX_DIST / exact) * (half - exact)).astype(np.int32)
    large = np.minimum(large, half - 1)
    return (ret + np.where(n < exact, n, large)).astype(np.int32)


def _attn_kernel(tab_ref, bkt_ref, q_ref, kp_ref, kc_ref, kn_ref, vp_ref, vc_ref, vn_ref,
                 o_ref, lse_ref, bias_ref, *, tq, dil, geom):
    r = pl.program_id(0)
    qb = pl.program_id(1)
    nk = tq + 2 * RADIUS

    @pl.when((r == 0) & (qb == 0))
    def _():
        bkt = bkt_ref[...]
        for hh in range(C_HPG):
            acc = jnp.full((tq, nk), NEG_INF, F32)
            for b in range(N_BUCKETS):
                acc = jnp.where(bkt == b, tab_ref[b * C_HPG + hh], acc)
            bias_ref[hh] = acc

    lo, hi = _seq_bounds(qb * tq * dil, geom)
    krow = qb * tq - RADIUS + lax.broadcasted_iota(jnp.int32, (1, nk), 1)
    in_seq = (krow * dil >= lo) & (krow * dil < hi)
    scale = HEAD_DIM ** -0.5
    for hh in range(C_HPG):
        sl = slice(hh * HEAD_DIM, (hh + 1) * HEAD_DIM)
        kwin = jnp.concatenate([kp_ref[tq - RADIUS:, sl], kc_ref[:, sl], kn_ref[:RADIUS, sl]], axis=0)
        vwin = jnp.concatenate([vp_ref[tq - RADIUS:, sl], vc_ref[:, sl], vn_ref[:RADIUS, sl]], axis=0)
        sc = _dot_nt(q_ref[:, sl], kwin) * scale + bias_ref[hh]
        sc = jnp.where(in_seq, sc, NEG_INF)
        m = jnp.max(sc, axis=-1, keepdims=True)
        p = jnp.exp(sc - m)
        den = jnp.sum(p, axis=-1, keepdims=True)
        o = _dot(p.astype(BF16), vwin) / den
        o_ref[:, sl] = o.astype(o_ref.dtype)
        lse_ref[:, sl] = jnp.broadcast_to(m + jnp.log(den), (tq, HEAD_DIM))


def _attn_group(src, col0, rel_bias, gi, geom, *, tq=128):
    dil, rows, _ = src.shape
    assert dil == C_GROUPS[gi][1] and C_GROUPS[gi][0] // (2 * dil) == RADIUS
    tq = min(tq, min(geom[1], geom[2]) // dil)
    assert tq >= RADIUS and rows % tq == 0 and (geom[1] // dil) % tq == 0 and (geom[2] // dil) % tq == 0
    nq = rows // tq
    nk = tq + 2 * RADIUS
    rel = np.arange(nk)[None, :] - RADIUS - np.arange(tq)[:, None]
    bkt = np.where(np.abs(rel) <= RADIUS, _t5_bucket(rel * dil), -1).astype(np.int32)
    tab = rel_bias[:, gi * C_HPG:(gi + 1) * C_HPG].astype(F32).reshape(-1)
    cw = C_OUT

    def cur(col):
        return lambda r, qb: (r, qb, col)

    def prev(col):
        return lambda r, qb: (r, jnp.maximum(qb - 1, 0), col)

    def nxt(col):
        return lambda r, qb: (r, jnp.minimum(qb + 1, nq - 1), col)

    blk = (None, tq, cw)
    qcol, kcol, vcol = col0, col0 + 1, col0 + 2
    kern = functools.partial(_attn_kernel, tq=tq, dil=dil, geom=geom)
    return pl.pallas_call(
        kern,
        out_shape=(jax.ShapeDtypeStruct((dil, rows, cw), BF16), jax.ShapeDtypeStruct((dil, rows, cw), F32)),
        grid=(dil, nq),
        in_specs=[
            pl.BlockSpec(memory_space=pltpu.SMEM),
            pl.BlockSpec((tq, nk), lambda r, qb: (0, 0)),
            pl.BlockSpec(blk, cur(qcol)),
            pl.BlockSpec(blk, prev(kcol)), pl.BlockSpec(blk, cur(kcol)), pl.BlockSpec(blk, nxt(kcol)),
            pl.BlockSpec(blk, prev(vcol)), pl.BlockSpec(blk, cur(vcol)), pl.BlockSpec(blk, nxt(vcol)),
        ],
        out_specs=(pl.BlockSpec(blk, cur(0)), pl.BlockSpec(blk, cur(0))),
        scratch_shapes=[pltpu.VMEM((C_HPG, tq, nk), F32)],
        compiler_params=pltpu.CompilerParams(dimension_semantics=("arbitrary", "arbitrary")),
        name=f"attn_g{gi}",
    )(tab, jnp.asarray(bkt), src, src, src, src, src, src, src)


def _deint_kernel(x1_ref, x2_ref, o1_ref, o2_ref, scr_ref, *, tm):
    for x_ref, o_ref in ((x1_ref, o1_ref), (x2_ref, o2_ref)):
        dil = o_ref.shape[0]
        for cb in range(scr_ref.shape[0]):
            cols = slice(cb * LANE, (cb + 1) * LANE)
            scr_ref[cb] = x_ref[:, cols].astype(F32)
            for r in range(dil):
                o_ref[r, :, cols] = scr_ref[cb, pl.ds(r, tm // dil, stride=dil), :].astype(o_ref.dtype)


def _deinterleave(p, *, tm=512):
    t = p.shape[0]
    tm = min(tm, t)
    w = 3 * C_OUT
    d1, d2 = C_GROUPS[1][1], C_GROUPS[2][1]
    kern = functools.partial(_deint_kernel, tm=tm)
    return pl.pallas_call(
        kern,
        out_shape=(jax.ShapeDtypeStruct((d1, t // d1, w), BF16), jax.ShapeDtypeStruct((d2, t // d2, w), BF16)),
        grid=(t // tm,),
        in_specs=[pl.BlockSpec((tm, w), lambda i: (i, P_CQKV // w + 1)),
                  pl.BlockSpec((tm, w), lambda i: (i, P_CQKV // w + 2))],
        out_specs=(pl.BlockSpec((d1, tm // d1, w), lambda i: (0, i, 0)),
                   pl.BlockSpec((d2, tm // d2, w), lambda i: (0, i, 0))),
        scratch_shapes=[pltpu.VMEM((w // LANE, tm, LANE), F32)],
        compiler_params=pltpu.CompilerParams(dimension_semantics=("parallel",)),
        name="deinterleave",
    )(p, p)


def _mix_prep_kernel(of_ref, ob_ref, z_ref, hn_ref, b_ref, bp_ref, bn_ref, cw_ref,
                     o0_ref, o1_ref, o2_ref, l0_ref, l1_ref, l2_ref, y_ref, so1, so2, sl1, sl2, *, tm, geom, halo):
    o = of_ref[...].astype(F32) + ob_ref[...].astype(F32)
    z = z_ref[...].astype(F32)
    for hd in range(A_HEADS):
        sl = slice(hd * HEAD_DIM, (hd + 1) * HEAD_DIM)
        oh = o[:, sl]
        zh = z[:, sl]
        y = _rms(oh, hn_ref[...]) * (zh * _sigmoid(zh))
        y_ref[:, sl] = y.astype(BF16)

    has_prev, has_next = _halo_flags(tm, geom)

    def gated(ref, rows):
        return ref[rows, B_WIDTH:2 * B_WIDTH].astype(F32) * ref[rows, 2 * B_WIDTH:].astype(F32)

    u = gated(b_ref, slice(None))
    prev_row = gated(bp_ref, slice(halo - 1, halo)) * has_prev
    next_row = gated(bn_ref, slice(0, 1)) * has_next
    yb = b_ref[:, :B_WIDTH].astype(F32) * _conv3(u, prev_row, next_row, cw_ref)
    y_ref[:, A_WIDTH:A_WIDTH + B_WIDTH] = yb.astype(BF16)

    for src_ref, dst_ref in ((o1_ref, so1), (o2_ref, so2), (l1_ref, sl1), (l2_ref, sl2)):
        dil = src_ref.shape[0]
        for hd in range(C_HPG):
            for r in range(dil):
                dst_ref[hd, pl.ds(r, tm // dil, stride=dil), :] = (
                    src_ref[r, :, hd * HEAD_DIM:(hd + 1) * HEAD_DIM].astype(F32))
    for hd in range(C_HPG):
        sl = slice(hd * HEAD_DIM, (hd + 1) * HEAD_DIM)
        l0, l1, l2 = l0_ref[:, sl], sl1[hd], sl2[hd]
        m = jnp.maximum(jnp.maximum(l0, l1), l2)
        e0, e1, e2 = jnp.exp(l0 - m), jnp.exp(l1 - m), jnp.exp(l2 - m)
        num = e0 * o0_ref[:, sl].astype(F32) + e1 * so1[hd] + e2 * so2[hd]
        y_ref[:, A_WIDTH + B_WIDTH + hd * HEAD_DIM:A_WIDTH + B_WIDTH + (hd + 1) * HEAD_DIM] = (
            num / (e0 + e1 + e2)).astype(BF16)


def _mix_prep(p, o_f, o_b, attn, head_norm, conv_b, geom, *, tm=256, halo=16):
    t = p.shape[0]
    tm = min(tm, t)
    nb = tm // halo
    last = t // halo - 1
    w3 = 3 * B_WIDTH
    row = lambda w, col=0: pl.BlockSpec((tm, w), lambda i: (i, col))
    kern = functools.partial(_mix_prep_kernel, tm=tm, geom=geom, halo=halo)
    (o0, l0), (o1, l1), (o2, l2) = attn

    def res(a):
        dil = a.shape[0]
        if dil == 1:
            return pl.BlockSpec((None, tm, C_OUT), lambda i: (0, i, 0))
        return pl.BlockSpec((dil, tm // dil, C_OUT), lambda i: (0, i, 0))

    vmem = 2 * tm * (2 * A_WIDTH * 2 + A_WIDTH * 2 + w3 * 2 + 3 * C_OUT * 6 + Y_WIDTH * 2) + 12 * tm * A_WIDTH * 4
    return pl.pallas_call(
        kern,
        out_shape=jax.ShapeDtypeStruct((t, Y_WIDTH), BF16),
        grid=(t // tm,),
        in_specs=[
            row(A_WIDTH), row(A_WIDTH), row(A_WIDTH, P_Z // A_WIDTH),
            pl.BlockSpec((1, HEAD_DIM), lambda i: (0, 0)),
            row(w3, P_B3 // w3),
            pl.BlockSpec((halo, w3), lambda i: (jnp.maximum(i * nb - 1, 0), P_B3 // w3)),
            pl.BlockSpec((halo, w3), lambda i: (jnp.minimum((i + 1) * nb, last), P_B3 // w3)),
            pl.BlockSpec((3, B_WIDTH), lambda i: (0, 0)),
            res(o0), res(o1), res(o2), res(l0), res(l1), res(l2),
        ],
        out_specs=row(Y_WIDTH),
        scratch_shapes=[pltpu.VMEM((C_HPG, tm, HEAD_DIM), F32)] * 4,
        compiler_params=pltpu.CompilerParams(
            dimension_semantics=("parallel",), vmem_limit_bytes=_vmem_limit(vmem)),
        name="mix_prep",
    )(o_f, o_b, p, head_norm.reshape(1, HEAD_DIM), p, p, p, conv_b, o0, o1, o2, l0, l1, l2)


def _merge_kernel(y_ref, g_ref, h_ref, wa_ref, wb_ref, wc_ref, wo_ref, o_ref):
    d = D_MODEL
    merged = _sigmoid(g_ref[:, 0:d].astype(F32)) * _dot(y_ref[:, :A_WIDTH], wa_ref[...])
    merged += _sigmoid(g_ref[:, d:2 * d].astype(F32)) * _dot(y_ref[:, A_WIDTH:A_WIDTH + B_WIDTH], wb_ref[...])
    merged += _sigmoid(g_ref[:, 2 * d:].astype(F32)) * _dot(y_ref[:, A_WIDTH + B_WIDTH:], wc_ref[...])
    o_ref[...] = h_ref[...] + _dot(merged.astype(BF16), wo_ref[...])


def _merge(y, p, h, wa, wb, wc, wo, *, tm=256):
    t = h.shape[0]
    tm = min(tm, t)
    d = D_MODEL
    const = lambda shape: pl.BlockSpec(shape, lambda i: (0, 0), pipeline_mode=pl.Buffered(1))
    wbytes = (A_WIDTH + B_WIDTH + C_OUT + d) * d * 2
    vmem = wbytes + 2 * tm * (Y_WIDTH * 2 + 3 * d * 2 + d * 4 + d * 4) + 6 * tm * d * 4
    return pl.pallas_call(
        _merge_kernel,
        out_shape=jax.ShapeDtypeStruct((t, d), F32),
        grid=(t // tm,),
        in_specs=[
            pl.BlockSpec((tm, Y_WIDTH), lambda i: (i, 0)),
            pl.BlockSpec((tm, 3 * d), lambda i: (i, P_GATES // (3 * d))),
            pl.BlockSpec((tm, d), lambda i: (i, 0)),
            const((A_WIDTH, d)), const((B_WIDTH, d)), const((C_OUT, d)), const((d, d)),
        ],
        out_specs=pl.BlockSpec((tm, d), lambda i: (i, 0)),
        compiler_params=pltpu.CompilerParams(
            dimension_semantics=("parallel",), vmem_limit_bytes=_vmem_limit(vmem)),
        name="merge",
    )(y, p, h, wa, wb, wc, wo)


def _mlp_kernel(h_ref, nw_ref, wu_ref, wd_ref, fw_ref, o_ref, xn_ref, acc_ref, *, final):
    f = pl.program_id(1)

    @pl.when(f == 0)
    def _():
        xn_ref[...] = _rms(h_ref[...], nw_ref[...]).astype(BF16)
        acc_ref[...] = jnp.zeros_like(acc_ref)

    up = jnp.maximum(_dot(xn_ref[...], wu_ref[...]), 0.0)
    acc_ref[...] += _dot((up * up).astype(BF16), wd_ref[...])

    @pl.when(f == pl.num_programs(1) - 1)
    def _():
        hn = h_ref[...] + acc_ref[...]
        o_ref[...] = _rms(hn, fw_ref[...]) if final else hn


def _mlp(h, norm_w, wu, wd, final_w, *, final, tm=512, tf=1024):
    t = h.shape[0]
    tm = min(tm, t)
    d = D_MODEL
    kern = functools.partial(_mlp_kernel, final=final)
    vmem = 4 * tm * d * 4 + tm * d * 2 + tm * d * 4 + 4 * d * tf * 2 + 3 * tm * tf * 4
    return pl.pallas_call(
        kern,
        out_shape=jax.ShapeDtypeStruct((t, d), F32),
        grid=(t // tm, D_FF // tf),
        in_specs=[
            pl.BlockSpec((tm, d), lambda i, f: (i, 0)),
            pl.BlockSpec((1, d), lambda i, f: (0, 0)),
            pl.BlockSpec((d, tf), lambda i, f: (0, f)),
            pl.BlockSpec((tf, d), lambda i, f: (f, 0)),
            pl.BlockSpec((1, d), lambda i, f: (0, 0)),
        ],
        out_specs=pl.BlockSpec((tm, d), lambda i, f: (i, 0)),
        scratch_shapes=[pltpu.VMEM((tm, d), BF16), pltpu.VMEM((tm, d), F32)],
        compiler_params=pltpu.CompilerParams(
            dimension_semantics=("parallel", "arbitrary"), vmem_limit_bytes=_vmem_limit(vmem + (8 << 20))),
        name="mlp",
    )(h, norm_w.reshape(1, d), wu, wd, final_w.reshape(1, d))


def _pack_w_in(w):
    d = w.shape[0]
    a_qkv, a_z = w[:, :3072], w[:, 3072:4096]
    ab = w[:, 4096:4128]
    b3 = w[:, 4128:7200]
    c0 = 7200
    cqkv = [w[:, c0 + part * C_WIDTH + g * C_OUT:c0 + part * C_WIDTH + (g + 1) * C_OUT]
            for g in range(len(C_GROUPS)) for part in range(3)]
    gates = w[:, 11808:]
    zeros = lambda n: jnp.zeros((d, n), w.dtype)
    packed = jnp.concatenate(
        [a_qkv, b3, a_z, ab, zeros(P_CQKV - P_AB - 32)] + cqkv + [gates], axis=1)
    assert packed.shape[1] == P_WIDTH
    return packed.astype(BF16)


def _trunk(h, geom, rel_bias, norm_mix, w_in, conv_a, a_log, dt_bias, head_norm, conv_b,
           w_br_a, w_br_b, w_br_c, w_out, norm_mlp, w_up, w_down, norm_final):
    depth = w_in.shape[0]
    t = h.shape[0]
    for layer in range(depth):
        p, ab = _in_proj(h, norm_mix[layer], _pack_w_in(w_in[layer]))
        qkv, gb = _gdn_pre(p, ab, conv_a[layer], a_log[layer], dt_bias[layer], geom)
        gbt3 = gb[:, :32].reshape(t // GDN_CHUNK, GDN_CHUNK, 32).transpose(0, 2, 1)
        o_f, o_b = _gdn_scan(qkv, gb, gbt3, geom)
        r1, r2 = _deinterleave(p)
        attn = [_attn_group(p.reshape(1, t, P_WIDTH), P_CQKV // C_OUT, rel_bias, 0, geom),
                _attn_group(r1, 0, rel_bias, 1, geom),
                _attn_group(r2, 0, rel_bias, 2, geom)]
        y = _mix_prep(p, o_f, o_b, attn, head_norm[layer], conv_b[layer], geom)
        h = _merge(y, p, h, w_br_a[layer].astype(BF16), w_br_b[layer].astype(BF16),
                   w_br_c[layer].astype(BF16), w_out[layer].astype(BF16))
        h = _mlp(h, norm_mlp[layer], w_up[layer].astype(BF16), w_down[layer].astype(BF16),
                 norm_final, final=(layer == depth - 1))
    return h


def kernel(x_prompt, x_sample, rel_bias, norm_mix, w_in, conv_a, a_log, dt_bias, head_norm, conv_b,
           w_br_a, w_br_b, w_br_c, w_out, norm_mlp, w_up, w_down, norm_final):
    bp, sp, d = x_prompt.shape
    bs, ss, _ = x_sample.shape
    tp = bp * sp
    geom = (tp, sp, ss)
    h = jnp.concatenate([x_prompt.reshape(tp, d), x_sample.reshape(bs * ss, d)], axis=0)
    y = _trunk(h, geom, rel_bias, norm_mix, w_in, conv_a, a_log, dt_bias, head_norm, conv_b,
               w_br_a, w_br_b, w_br_c, w_out, norm_mlp, w_up, w_down, norm_final)
    return y[:tp].reshape(bp, sp, d), y[tp:].reshape(bs, ss, d)
```

```python
import functools
import math

import jax
import jax.numpy as jnp
import numpy as np
from jax import lax
from jax.experimental import pallas as pl
from jax.experimental.pallas import tpu as pltpu

F32 = jnp.float32
BF16 = jnp.bfloat16

LANE = 128
VMEM_BYTES_V7X = 64 * 1024 * 1024

D_MODEL = 2048
A_HEADS = 8
HEAD_DIM = 128
A_WIDTH = A_HEADS * HEAD_DIM
GDN_CHUNK = 64
B_WIDTH = 1024
C_GROUPS = ((128, 1), (512, 4), (2048, 16))
C_HPG = 4
C_HEADS = 12
C_WIDTH = C_HEADS * HEAD_DIM
C_OUT = C_HPG * HEAD_DIM
N_BUCKETS = 32
REL_MAX_DIST = 2048
RADIUS = 64
D_FF = 4 * D_MODEL
EPS = 1e-6
NEG_INF = -1e30

P_QKV = 0
P_B3 = 3072
P_Z = 6144
P_AB = 7168
P_CQKV = 7680
P_GATES = 12288
P_WIDTH = 18432
Y_WIDTH = A_WIDTH + B_WIDTH + C_OUT


def _vmem_limit(nbytes):
    return int(min(max(nbytes, 16 * 1024 * 1024), VMEM_BYTES_V7X - 8 * 1024 * 1024))


def _seq_bounds(t0, geom):
    tp, sp, ss = geom
    in_p = t0 < tp
    length = jnp.where(in_p, sp, ss)
    base = jnp.where(in_p, 0, tp)
    lo = base + ((t0 - base) // length) * length
    return lo, lo + length


def _rms(x, w):
    ms = jnp.mean(x * x, axis=-1, keepdims=True)
    return x * lax.rsqrt(ms + EPS) * w


def _sigmoid(x):
    return 1.0 / (1.0 + jnp.exp(-x))


def _dot(a, b):
    return jnp.dot(a, b, preferred_element_type=F32)


def _dot_nt(a, b):
    return lax.dot_general(a, b, (((1,), (1,)), ((), ())), preferred_element_type=F32)


def _dot_tn(a, b):
    return lax.dot_general(a, b, (((0,), (0,)), ((), ())), preferred_element_type=F32)


def _row_specs(srcs, tm, width, single_buffer=False):
    specs, bounds = [], [0]
    for a in srcs:
        n, off = a.shape[0] // tm, bounds[-1]
        assert a.shape[0] % tm == 0
        mode = dict(pipeline_mode=pl.Buffered(1)) if single_buffer and len(srcs) > 1 else {}
        specs.append(pl.BlockSpec((tm, width), lambda i, *_, off=off, n=n: (jnp.clip(i - off, 0, n - 1), 0), **mode))
        bounds.append(off + n)
    return specs, tuple(bounds)


def _on_owner(refs, bounds, fn, extra_cond=None):
    i = pl.program_id(0)
    for k, ref in enumerate(refs):
        cond = extra_cond
        if len(refs) > 1:
            own = (i >= bounds[k]) & (i < bounds[k + 1])
            cond = own if cond is None else cond & own
        if cond is None:
            fn(ref)
        else:
            pl.when(cond)(functools.partial(fn, ref))


def _inproj_kernel(*refs, bounds, ab_tile, ab_off):
    nsrc = len(bounds) - 1
    x_refs = refs[:nsrc]
    nw_ref, w_ref, p_ref, ab_ref, xn_ref = refs[nsrc:]
    j = pl.program_id(1)

    def norm(x_ref):
        xn_ref[...] = _rms(x_ref[...], nw_ref[...]).astype(BF16)

    _on_owner(x_refs, bounds, norm, extra_cond=(j == 0))
    acc = _dot(xn_ref[...], w_ref[...])
    p_ref[...] = acc.astype(BF16)

    @pl.when(j == ab_tile)
    def _():
        ab_ref[...] = acc[:, ab_off:ab_off + LANE]


def _in_proj(hs, norm_w, w_packed, *, tm=1024, tn=1024):
    t = sum(a.shape[0] for a in hs)
    tm = min([tm] + [a.shape[0] for a in hs])
    assert P_WIDTH % tn == 0
    x_specs, bounds = _row_specs(hs, tm, D_MODEL, single_buffer=True)
    kern = functools.partial(_inproj_kernel, bounds=bounds, ab_tile=P_AB // tn, ab_off=P_AB % tn)
    vmem = 2 * tm * D_MODEL * 4 + tm * D_MODEL * 2 + 2 * D_MODEL * tn * 2 + 2 * tm * tn * 2 + 2 * tm * tn * 4
    return pl.pallas_call(
        kern,
        out_shape=(jax.ShapeDtypeStruct((t, P_WIDTH), BF16), jax.ShapeDtypeStruct((t, LANE), F32)),
        grid=(t // tm, P_WIDTH // tn),
        in_specs=x_specs + [
            pl.BlockSpec((1, D_MODEL), lambda i, j: (0, 0)),
            pl.BlockSpec((D_MODEL, tn), lambda i, j: (0, j)),
        ],
        out_specs=(
            pl.BlockSpec((tm, tn), lambda i, j: (i, j)),
            pl.BlockSpec((tm, LANE), lambda i, j: (i, 0)),
        ),
        scratch_shapes=[pltpu.VMEM((tm, D_MODEL), BF16)],
        compiler_params=pltpu.CompilerParams(
            dimension_semantics=("parallel", "arbitrary"), vmem_limit_bytes=_vmem_limit(vmem + (8 << 20))),
        name="in_proj",
    )(*hs, norm_w.reshape(1, D_MODEL), w_packed)


def _conv3(x, prev_row, next_row, w_ref):
    tm = x.shape[0]
    rows = lax.broadcasted_iota(jnp.int32, (tm, 1), 0)
    x_m1 = jnp.where(rows == 0, prev_row, pltpu.roll(x, 1, 0))
    x_p1 = jnp.where(rows == tm - 1, next_row, pltpu.roll(x, tm - 1, 0))
    return x_m1 * w_ref[0:1, :] + x * w_ref[1:2, :] + x_p1 * w_ref[2:3, :]


def _halo_flags(tm, geom):
    t0 = pl.program_id(0) * tm
    lo, hi = _seq_bounds(t0, geom)
    return (t0 > lo).astype(F32), (t0 + tm < hi).astype(F32)


def _gdn_pre_kernel(x_ref, xp_ref, xn_ref, cw_ref, ab_ref, nalog_ref, dtb_ref, o_ref, gb_ref, *, tm, geom, halo):
    has_prev, has_next = _halo_flags(tm, geom)
    x = x_ref[...].astype(F32)
    prev_row = xp_ref[halo - 1:halo, :].astype(F32) * has_prev
    next_row = xn_ref[0:1, :].astype(F32) * has_next
    y = _conv3(x, prev_row, next_row, cw_ref)
    y = y * _sigmoid(y)
    for hd in range(2 * A_HEADS):
        sl = slice(hd * HEAD_DIM, (hd + 1) * HEAD_DIM)
        yh = y[:, sl]
        inv = lax.rsqrt(jnp.sum(yh * yh, axis=-1, keepdims=True) + EPS)
        if hd < A_HEADS:
            inv = inv * (HEAD_DIM ** -0.5)
        o_ref[:, sl] = (yh * inv).astype(BF16)
    o_ref[:, 2 * A_WIDTH:] = y[:, 2 * A_WIDTH:].astype(BF16)

    ab = ab_ref[...]
    xs = ab + dtb_ref[...]
    softplus = jnp.maximum(xs, 0.0) + jnp.log(1.0 + jnp.exp(-jnp.abs(xs)))
    g = nalog_ref[...] * softplus
    lane = lax.broadcasted_iota(jnp.int32, ab.shape, 1)
    gb_ref[...] = jnp.where(lane < 2 * A_HEADS, g, _sigmoid(ab))


def _gdn_pre(p, ab, conv_w, a_log, dt_bias, geom, *, tm=256, halo=16):
    t = p.shape[0]
    tm = min(tm, t)
    w = 3 * A_WIDTH
    nb = tm // halo
    last = t // halo - 1
    nalog = jnp.zeros((1, LANE), F32).at[0, :2 * A_HEADS].set(-jnp.exp(a_log.reshape(-1)))
    dtb = jnp.zeros((1, LANE), F32).at[0, :2 * A_HEADS].set(dt_bias.reshape(-1))
    kern = functools.partial(_gdn_pre_kernel, tm=tm, geom=geom, halo=halo)
    vmem = 2 * (tm * w * 2 * 2 + 2 * halo * w * 2) + 10 * tm * w * 4
    return pl.pallas_call(
        kern,
        out_shape=(jax.ShapeDtypeStruct((t, w), BF16), jax.ShapeDtypeStruct((t, LANE), F32)),
        grid=(t // tm,),
        in_specs=[
            pl.BlockSpec((tm, w), lambda i: (i, P_QKV // w)),
            pl.BlockSpec((halo, w), lambda i: (jnp.maximum(i * nb - 1, 0), P_QKV // w)),
            pl.BlockSpec((halo, w), lambda i: (jnp.minimum((i + 1) * nb, last), P_QKV // w)),
            pl.BlockSpec((3, w), lambda i: (0, 0)),
            pl.BlockSpec((tm, LANE), lambda i: (i, 0)),
            pl.BlockSpec((1, LANE), lambda i: (0, 0)),
            pl.BlockSpec((1, LANE), lambda i: (0, 0)),
        ],
        out_specs=(
            pl.BlockSpec((tm, w), lambda i: (i, 0)),
            pl.BlockSpec((tm, LANE), lambda i: (i, 0)),
        ),
        compiler_params=pltpu.CompilerParams(
            dimension_semantics=("parallel",), vmem_limit_bytes=_vmem_limit(vmem)),
        name="gdn_pre",
    )(p, p, p, conv_w, ab, nalog, dtb)


def _split3(x):
    x1 = x.astype(BF16)
    r1 = x - x1.astype(F32)
    x2 = r1.astype(BF16)
    r2 = r1 - x2.astype(F32)
    return x1, x2, r2.astype(BF16)


def _gdn_scan_kernel(qf_ref, kf_ref, vf_ref, gf_ref, grf_ref, qb_ref, kb_ref, vb_ref, gbk_ref, grb_ref,
                     of_ref, ob_ref, s_ref, wq_s, u_s, lhs2_s, egl_s, *, rb, nblk, geom):
    c = GDN_CHUNK
    nc = rb // c
    i = pl.program_id(0)
    t0f = i * rb
    t0b = (nblk - 1 - i) * rb
    lo_f, _ = _seq_bounds(t0f, geom)
    _, hi_b = _seq_bounds(t0b, geom)

    @pl.when(t0f == lo_f)
    def _():
        s_ref[0:A_HEADS] = jnp.zeros((A_HEADS, HEAD_DIM, HEAD_DIM), F32)

    @pl.when(t0b + rb == hi_b)
    def _():
        s_ref[A_HEADS:2 * A_HEADS] = jnp.zeros((A_HEADS, HEAD_DIM, HEAD_DIM), F32)

    ri = lax.broadcasted_iota(jnp.int32, (c, c), 0)
    ci = lax.broadcasted_iota(jnp.int32, (c, c), 1)
    lower_incl = ri >= ci
    upper_incl = ri <= ci
    tri_lo = jnp.where(lower_incl, 1.0, 0.0).astype(BF16)
    tri_up = jnp.where(upper_incl, 1.0, 0.0).astype(BF16)
    bd16 = (ri // 16) == (ci // 16)
    off32 = ((ri // 32) == (ci // 32)) & jnp.logical_not(bd16)
    off64 = (ri // 32) != (ci // 32)
    eye = jnp.where(ri == ci, 1.0, 0.0).astype(F32)
    bf = lambda x: x.astype(BF16)

    sides = (
        (0, qf_ref, kf_ref, vf_ref, gf_ref, grf_ref, tri_lo, tri_up, lower_incl, ri > ci),
        (1, qb_ref, kb_ref, vb_ref, gbk_ref, grb_ref, tri_up, tri_lo, upper_incl, ri < ci),
    )

    def prep(jp, carry):
        chains = []
        for direction, q_ref, k_ref, v_ref, g_ref, gr_ref, m_col, m_row, incl, strict in sides:
            for cidx in (2 * jp, 2 * jp + 1):
                r0 = pl.multiple_of(cidx * c, c)
                gcol = g_ref[pl.ds(r0, c), :]
                grow = gr_ref[cidx]
                g1, g2, g3 = _split3(gcol)
                gc_col = _dot(m_col, g1) + _dot(m_col, g2) + _dot(m_col, g3)
                h1, h2, h3 = _split3(grow)
                gc_row = _dot(h1, m_row) + _dot(h2, m_row) + _dot(h3, m_row)
                for hh in range(A_HEADS):
                    ch = direction * A_HEADS + hh
                    sl = slice(hh * HEAD_DIM, (hh + 1) * HEAD_DIM)
                    gcr = gc_row[ch:ch + 1, :]
                    chains.append(dict(
                        ch=ch, cidx=cidx, incl=incl, strict=strict,
                        q=q_ref[pl.ds(r0, c), sl], k=k_ref[pl.ds(r0, c), sl], v=v_ref[pl.ds(r0, c), sl],
                        gcr=gcr, beta_row=grow[2 * A_HEADS + ch:2 * A_HEADS + ch + 1, :],
                        gcc_b=jnp.broadcast_to(gc_col[:, ch:ch + 1], (c, HEAD_DIM)),
                        g_last=gcr[:, c - 1:c] if direction == 0 else gcr[:, 0:1]))

        gram = [_dot_nt(jnp.concatenate([x["q"], x["k"]], axis=0), x["k"]) for x in chains]
        zs, e32, e64 = [], [], []
        for x, gm in zip(chains, gram):
            decay = jnp.exp(jnp.where(x["incl"], x["gcc_b"][:, :c] - x["gcr"], NEG_INF))
            dec_beta = decay * x["beta_row"]
            k_t = jnp.transpose(x["k"].astype(F32))
            k_dec_t = k_t * (jnp.exp(x["g_last"] - x["gcr"]) * x["beta_row"])
            lhs2_s[x["cidx"], x["ch"]] = jnp.concatenate([bf(gm[:c] * dec_beta), bf(k_dec_t)], axis=0)
            egl_s[x["cidx"], x["ch"]] = jnp.broadcast_to(jnp.exp(x["g_last"]), (1, HEAD_DIM))
            l_mat = jnp.where(x["strict"], gm[c:] * dec_beta, 0.0)
            zs.append(jnp.where(bd16, -l_mat, 0.0))
            e32.append(bf(jnp.where(off32, l_mat, 0.0)))
            e64.append(bf(jnp.where(off64, l_mat, 0.0)))

        zb = [bf(z) for z in zs]
        xs = [eye + z for z in zs]
        pw = [bf(_dot(b, b)) for b in zb]
        for step in range(3):
            nxt = [bf(_dot(p, p)) for p in pw] if step < 2 else None
            xs = [x + _dot(bf(x), p) for x, p in zip(xs, pw)]
            pw = nxt
        for e in (e32, e64):
            xb = [bf(x) for x in xs]
            ts = [bf(_dot(b, a)) for b, a in zip(xb, e)]
            xs = [x - _dot(t, b) for x, t, b in zip(xs, ts, xb)]

        for x, t_inv in zip(chains, xs):
            eg = jnp.exp(x["gcc_b"])
            rhs = jnp.concatenate([x["v"], bf(x["k"].astype(F32) * eg)], axis=1)
            uw = _dot(bf(t_inv), rhs)
            u_s[x["cidx"], x["ch"]] = uw[:, :HEAD_DIM]
            wq_s[x["cidx"], x["ch"]] = jnp.concatenate(
                [bf(uw[:, HEAD_DIM:]), bf(x["q"].astype(F32) * eg)], axis=0)
        return carry

    lax.fori_loop(0, nc // 2, prep, 0)

    def scan(j, carry):
        chains = [(d * A_HEADS + hh, cidx, o_ref, slice(hh * HEAD_DIM, (hh + 1) * HEAD_DIM))
                  for d, cidx, o_ref in ((0, j, of_ref), (1, nc - 1 - j, ob_ref)) for hh in range(A_HEADS)]
        states = [s_ref[ch] for ch, _, _, _ in chains]
        ws_qs = [_dot(wq_s[cidx, ch], bf(s)) for (ch, cidx, _, _), s in zip(chains, states)]
        v_new = [bf(u_s[cidx, ch] - r[:c]) for (ch, cidx, _, _), r in zip(chains, ws_qs)]
        upd = [_dot(lhs2_s[cidx, ch], vn) for (ch, cidx, _, _), vn in zip(chains, v_new)]
        for (ch, cidx, o_ref, sl), s, r, r2 in zip(chains, states, ws_qs, upd):
            r0 = pl.multiple_of(cidx * c, c)
            o_ref[pl.ds(r0, c), sl] = (r[c:] + r2[:c]).astype(o_ref.dtype)
            s_ref[ch] = s * egl_s[cidx, ch] + r2[c:]
        return carry

    lax.fori_loop(0, nc, scan, 0)


def _gdn_scan(qkv, gb, gbt3, geom, *, rb=512):
    t = qkv.shape[0]
    rb = min(rb, t)
    nblk = t // rb
    ncb = rb // GDN_CHUNK
    kern = functools.partial(_gdn_scan_kernel, rb=rb, nblk=nblk, geom=geom)

    def fwd(col):
        return lambda i: (i, col)

    def bwd(col):
        return lambda i: (nblk - 1 - i, col)

    in_specs = []
    for mk in (fwd, bwd):
        in_specs += [
            pl.BlockSpec((rb, A_WIDTH), mk(0)),
            pl.BlockSpec((rb, A_WIDTH), mk(1)),
            pl.BlockSpec((rb, A_WIDTH), mk(2)),
            pl.BlockSpec((rb, LANE), mk(0)),
            pl.BlockSpec((ncb, 32, GDN_CHUNK), (lambda i: (i, 0, 0)) if mk is fwd else (lambda i: (nblk - 1 - i, 0, 0))),
        ]
    nch = 2 * A_HEADS
    assert ncb % 2 == 0
    vmem = (2 * (6 * rb * A_WIDTH * 2 + 2 * rb * LANE * 4 + 2 * rb * A_WIDTH * 2) + nch * HEAD_DIM * HEAD_DIM * 4
            + ncb * nch * HEAD_DIM * (2 * GDN_CHUNK * 2 + GDN_CHUNK * 4 + (GDN_CHUNK + HEAD_DIM) * 2 + 8 * 4))
    return pl.pallas_call(
        kern,
        out_shape=(jax.ShapeDtypeStruct((t, A_WIDTH), BF16), jax.ShapeDtypeStruct((t, A_WIDTH), BF16)),
        grid=(nblk,),
        in_specs=in_specs,
        out_specs=(pl.BlockSpec((rb, A_WIDTH), fwd(0)), pl.BlockSpec((rb, A_WIDTH), bwd(0))),
        scratch_shapes=[
            pltpu.VMEM((nch, HEAD_DIM, HEAD_DIM), F32),
            pltpu.VMEM((ncb, nch, 2 * GDN_CHUNK, HEAD_DIM), BF16),
            pltpu.VMEM((ncb, nch, GDN_CHUNK, HEAD_DIM), F32),
            pltpu.VMEM((ncb, nch, GDN_CHUNK + HEAD_DIM, GDN_CHUNK), BF16),
            pltpu.VMEM((ncb, nch, 1, HEAD_DIM), F32),
        ],
        compiler_params=pltpu.CompilerParams(
            dimension_semantics=("arbitrary",), vmem_limit_bytes=_vmem_limit(vmem + (16 << 20))),
        name="gdn_scan",
    )(qkv, qkv, qkv, gb, gbt3, qkv, qkv, qkv, gb, gbt3)


def _t5_bucket(rel):
    half = N_BUCKETS // 2
    exact = half // 2
    ret = np.where(rel > 0, half, 0)
    n = np.abs(rel)
    large = exact + (np.log(np.maximum(n, 1) / exact) / math.log(REL_MAX_DIST / exact) * (half - exact)).astype(np.int32)
    large = np.minimum(large, half - 1)
    return (ret + np.where(n < exact, n, large)).astype(np.int32)


def _attn_kernel(tab_ref, bkt_ref, q_ref, kp_ref, kc_ref, kn_ref, vp_ref, vc_ref, vn_ref,
                 o_ref, lse_ref, bias_ref, *, tq, dil, geom):
    r = pl.program_id(0)
    qb = pl.program_id(1)
    nk = tq + 2 * RADIUS

    @pl.when((r == 0) & (qb == 0))
    def _():
        bkt = bkt_ref[...]
        for hh in range(C_HPG):
            acc = jnp.full((tq, nk), NEG_INF, F32)
            for b in range(N_BUCKETS):
                acc = jnp.where(bkt == b, tab_ref[b * C_HPG + hh], acc)
            bias_ref[hh] = acc

    lo, hi = _seq_bounds(qb * tq * dil, geom)
    krow = qb * tq - RADIUS + lax.broadcasted_iota(jnp.int32, (1, nk), 1)
    in_seq = (krow * dil >= lo) & (krow * dil < hi)
    scale = HEAD_DIM ** -0.5
    heads = [slice(hh * HEAD_DIM, (hh + 1) * HEAD_DIM) for hh in range(C_HPG)]

    def window(p_ref, c_ref, n_ref, sl):
        return jnp.concatenate([p_ref[tq - RADIUS:, sl], c_ref[:, sl], n_ref[:RADIUS, sl]], axis=0)

    scores = [_dot_nt(q_ref[:, sl], window(kp_ref, kc_ref, kn_ref, sl)) for sl in heads]
    probs, dens = [], []
    for hh, sl in enumerate(heads):
        sc = jnp.where(in_seq, scores[hh] * scale + bias_ref[hh], NEG_INF)
        m = jnp.max(sc, axis=-1, keepdims=True)
        p = jnp.exp(sc - m)
        den = jnp.sum(p, axis=-1, keepdims=True)
        lse_ref[:, sl] = jnp.broadcast_to(m + jnp.log(den), (tq, HEAD_DIM))
        probs.append(p.astype(BF16))
        dens.append(den)
    for hh, sl in enumerate(heads):
        o = _dot(probs[hh], window(vp_ref, vc_ref, vn_ref, sl)) / dens[hh]
        o_ref[:, sl] = o.astype(o_ref.dtype)


def _attn_group(src, col0, rel_bias, gi, geom, *, tq=256):
    dil, rows, _ = src.shape
    assert dil == C_GROUPS[gi][1] and C_GROUPS[gi][0] // (2 * dil) == RADIUS
    tq = min(tq, min(geom[1], geom[2]) // dil)
    assert tq >= RADIUS and rows % tq == 0 and (geom[1] // dil) % tq == 0 and (geom[2] // dil) % tq == 0
    nq = rows // tq
    nk = tq + 2 * RADIUS
    rel = np.arange(nk)[None, :] - RADIUS - np.arange(tq)[:, None]
    bkt = np.where(np.abs(rel) <= RADIUS, _t5_bucket(rel * dil), -1).astype(np.int32)
    tab = rel_bias[:, gi * C_HPG:(gi + 1) * C_HPG].astype(F32).reshape(-1)
    cw = C_OUT

    def cur(col):
        return lambda r, qb: (r, qb, col)

    def prev(col):
        return lambda r, qb: (r, jnp.maximum(qb - 1, 0), col)

    def nxt(col):
        return lambda r, qb: (r, jnp.minimum(qb + 1, nq - 1), col)

    blk = (None, tq, cw)
    qcol, kcol, vcol = col0, col0 + 1, col0 + 2
    kern = functools.partial(_attn_kernel, tq=tq, dil=dil, geom=geom)
    return pl.pallas_call(
        kern,
        out_shape=(jax.ShapeDtypeStruct((dil, rows, cw), BF16), jax.ShapeDtypeStruct((dil, rows, cw), F32)),
        grid=(dil, nq),
        in_specs=[
            pl.BlockSpec(memory_space=pltpu.SMEM),
            pl.BlockSpec((tq, nk), lambda r, qb: (0, 0)),
            pl.BlockSpec(blk, cur(qcol)),
            pl.BlockSpec(blk, prev(kcol)), pl.BlockSpec(blk, cur(kcol)), pl.BlockSpec(blk, nxt(kcol)),
            pl.BlockSpec(blk, prev(vcol)), pl.BlockSpec(blk, cur(vcol)), pl.BlockSpec(blk, nxt(vcol)),
        ],
        out_specs=(pl.BlockSpec(blk, cur(0)), pl.BlockSpec(blk, cur(0))),
        scratch_shapes=[pltpu.VMEM((C_HPG, tq, nk), F32)],
        compiler_params=pltpu.CompilerParams(dimension_semantics=("arbitrary", "arbitrary")),
        name=f"attn_g{gi}",
    )(tab, jnp.asarray(bkt), src, src, src, src, src, src, src)


def _deint_kernel(x1_ref, x2_ref, o1_ref, o2_ref, scr_ref, *, tm):
    for x_ref, o_ref in ((x1_ref, o1_ref), (x2_ref, o2_ref)):
        dil = o_ref.shape[0]
        for cb in range(scr_ref.shape[0]):
            cols = slice(cb * LANE, (cb + 1) * LANE)
            scr_ref[cb] = x_ref[:, cols].astype(F32)
            for r in range(dil):
                o_ref[r, :, cols] = scr_ref[cb, pl.ds(r, tm // dil, stride=dil), :].astype(o_ref.dtype)


def _deinterleave(p, *, tm=512):
    t = p.shape[0]
    tm = min(tm, t)
    w = 3 * C_OUT
    d1, d2 = C_GROUPS[1][1], C_GROUPS[2][1]
    kern = functools.partial(_deint_kernel, tm=tm)
    return pl.pallas_call(
        kern,
        out_shape=(jax.ShapeDtypeStruct((d1, t // d1, w), BF16), jax.ShapeDtypeStruct((d2, t // d2, w), BF16)),
        grid=(t // tm,),
        in_specs=[pl.BlockSpec((tm, w), lambda i: (i, P_CQKV // w + 1)),
                  pl.BlockSpec((tm, w), lambda i: (i, P_CQKV // w + 2))],
        out_specs=(pl.BlockSpec((d1, tm // d1, w), lambda i: (0, i, 0)),
                   pl.BlockSpec((d2, tm // d2, w), lambda i: (0, i, 0))),
        scratch_shapes=[pltpu.VMEM((w // LANE, tm, LANE), F32)],
        compiler_params=pltpu.CompilerParams(dimension_semantics=("parallel",)),
        name="deinterleave",
    )(p, p)


def _mix_prep_kernel(of_ref, ob_ref, z_ref, hn_ref, b_ref, bp_ref, bn_ref, cw_ref,
                     o0_ref, o1_ref, o2_ref, l0_ref, l1_ref, l2_ref, y_ref, so1, so2, sl1, sl2, *, tm, geom, halo):
    o = of_ref[...].astype(F32) + ob_ref[...].astype(F32)
    z = z_ref[...].astype(F32)
    for hd in range(A_HEADS):
        sl = slice(hd * HEAD_DIM, (hd + 1) * HEAD_DIM)
        oh = o[:, sl]
        zh = z[:, sl]
        y = _rms(oh, hn_ref[...]) * (zh * _sigmoid(zh))
        y_ref[:, sl] = y.astype(BF16)

    has_prev, has_next = _halo_flags(tm, geom)

    def gated(ref, rows):
        return ref[rows, B_WIDTH:2 * B_WIDTH].astype(F32) * ref[rows, 2 * B_WIDTH:].astype(F32)

    u = gated(b_ref, slice(None))
    prev_row = gated(bp_ref, slice(halo - 1, halo)) * has_prev
    next_row = gated(bn_ref, slice(0, 1)) * has_next
    yb = b_ref[:, :B_WIDTH].astype(F32) * _conv3(u, prev_row, next_row, cw_ref)
    y_ref[:, A_WIDTH:A_WIDTH + B_WIDTH] = yb.astype(BF16)

    for src_ref, dst_ref in ((o1_ref, so1), (o2_ref, so2), (l1_ref, sl1), (l2_ref, sl2)):
        dil = src_ref.shape[0]
        for hd in range(C_HPG):
            for r in range(dil):
                dst_ref[hd, pl.ds(r, tm // dil, stride=dil), :] = (
                    src_ref[r, :, hd * HEAD_DIM:(hd + 1) * HEAD_DIM].astype(F32))
    for hd in range(C_HPG):
        sl = slice(hd * HEAD_DIM, (hd + 1) * HEAD_DIM)
        l0, l1, l2 = l0_ref[:, sl], sl1[hd], sl2[hd]
        m = jnp.maximum(jnp.maximum(l0, l1), l2)
        e0, e1, e2 = jnp.exp(l0 - m), jnp.exp(l1 - m), jnp.exp(l2 - m)
        num = e0 * o0_ref[:, sl].astype(F32) + e1 * so1[hd] + e2 * so2[hd]
        y_ref[:, A_WIDTH + B_WIDTH + hd * HEAD_DIM:A_WIDTH + B_WIDTH + (hd + 1) * HEAD_DIM] = (
            num / (e0 + e1 + e2)).astype(BF16)


def _mix_prep(p, o_f, o_b, attn, head_norm, conv_b, geom, *, tm=256, halo=16):
    t = p.shape[0]
    tm = min(tm, t)
    nb = tm // halo
    last = t // halo - 1
    w3 = 3 * B_WIDTH
    row = lambda w, col=0: pl.BlockSpec((tm, w), lambda i: (i, col))
    kern = functools.partial(_mix_prep_kernel, tm=tm, geom=geom, halo=halo)
    (o0, l0), (o1, l1), (o2, l2) = attn

    def res(a):
        dil = a.shape[0]
        if dil == 1:
            return pl.BlockSpec((None, tm, C_OUT), lambda i: (0, i, 0))
        return pl.BlockSpec((dil, tm // dil, C_OUT), lambda i: (0, i, 0))

    vmem = 2 * tm * (2 * A_WIDTH * 2 + A_WIDTH * 2 + w3 * 2 + 3 * C_OUT * 6 + Y_WIDTH * 2) + 12 * tm * A_WIDTH * 4
    return pl.pallas_call(
        kern,
        out_shape=jax.ShapeDtypeStruct((t, Y_WIDTH), BF16),
        grid=(t // tm,),
        in_specs=[
            row(A_WIDTH), row(A_WIDTH), row(A_WIDTH, P_Z // A_WIDTH),
            pl.BlockSpec((1, HEAD_DIM), lambda i: (0, 0)),
            row(w3, P_B3 // w3),
            pl.BlockSpec((halo, w3), lambda i: (jnp.maximum(i * nb - 1, 0), P_B3 // w3)),
            pl.BlockSpec((halo, w3), lambda i: (jnp.minimum((i + 1) * nb, last), P_B3 // w3)),
            pl.BlockSpec((3, B_WIDTH), lambda i: (0, 0)),
            res(o0), res(o1), res(o2), res(l0), res(l1), res(l2),
        ],
        out_specs=row(Y_WIDTH),
        scratch_shapes=[pltpu.VMEM((C_HPG, tm, HEAD_DIM), F32)] * 4,
        compiler_params=pltpu.CompilerParams(
            dimension_semantics=("parallel",), vmem_limit_bytes=_vmem_limit(vmem)),
        name="mix_prep",
    )(o_f, o_b, p, head_norm.reshape(1, HEAD_DIM), p, p, p, conv_b, o0, o1, o2, l0, l1, l2)


def _merge_kernel(*refs, bounds):
    nsrc = len(bounds) - 1
    h_refs = refs[:nsrc]
    y_ref, g_ref, wa_ref, wb_ref, wc_ref, wo_ref, o_ref = refs[nsrc:]
    d = D_MODEL
    merged = _sigmoid(g_ref[:, 0:d].astype(F32)) * _dot(y_ref[:, :A_WIDTH], wa_ref[...])
    merged += _sigmoid(g_ref[:, d:2 * d].astype(F32)) * _dot(y_ref[:, A_WIDTH:A_WIDTH + B_WIDTH], wb_ref[...])
    merged += _sigmoid(g_ref[:, 2 * d:].astype(F32)) * _dot(y_ref[:, A_WIDTH + B_WIDTH:], wc_ref[...])
    delta = _dot(merged.astype(BF16), wo_ref[...])

    def residual(h_ref):
        o_ref[...] = h_ref[...] + delta

    _on_owner(h_refs, bounds, residual)


def _merge(y, p, hs, wa, wb, wc, wo, *, tm=256):
    t = y.shape[0]
    tm = min([tm] + [a.shape[0] for a in hs])
    d = D_MODEL
    h_specs, bounds = _row_specs(hs, tm, d)
    const = lambda shape: pl.BlockSpec(shape, lambda i: (0, 0), pipeline_mode=pl.Buffered(1))
    wbytes = (A_WIDTH + B_WIDTH + C_OUT + d) * d * 2
    vmem = wbytes + 2 * tm * (Y_WIDTH * 2 + 3 * d * 2 + len(hs) * d * 4 + d * 4) + 6 * tm * d * 4
    return pl.pallas_call(
        functools.partial(_merge_kernel, bounds=bounds),
        out_shape=jax.ShapeDtypeStruct((t, d), F32),
        grid=(t // tm,),
        in_specs=h_specs + [
            pl.BlockSpec((tm, Y_WIDTH), lambda i: (i, 0)),
            pl.BlockSpec((tm, 3 * d), lambda i: (i, P_GATES // (3 * d))),
            const((A_WIDTH, d)), const((B_WIDTH, d)), const((C_OUT, d)), const((d, d)),
        ],
        out_specs=pl.BlockSpec((tm, d), lambda i: (i, 0)),
        compiler_params=pltpu.CompilerParams(
            dimension_semantics=("parallel",), vmem_limit_bytes=_vmem_limit(vmem)),
        name="merge",
    )(*hs, y, p, wa, wb, wc, wo)


def _mlp_kernel(h_ref, nw_ref, wu_ref, wd_ref, fw_ref, *refs, bounds, final):
    nout = len(bounds) - 1
    o_refs = refs[:nout]
    xn_ref, acc_ref = refs[nout:]
    f = pl.program_id(1)

    @pl.when(f == 0)
    def _():
        xn_ref[...] = _rms(h_ref[...], nw_ref[...]).astype(BF16)
        acc_ref[...] = jnp.zeros_like(acc_ref)

    up = jnp.maximum(_dot(xn_ref[...], wu_ref[...]), 0.0)
    acc_ref[...] += _dot((up * up).astype(BF16), wd_ref[...])

    def emit(o_ref):
        hn = h_ref[...] + acc_ref[...]
        o_ref[...] = _rms(hn, fw_ref[...]) if final else hn

    _on_owner(o_refs, bounds, emit, extra_cond=(f == pl.num_programs(1) - 1))


def _mlp(h, norm_w, wu, wd, final_w, *, final, out_rows, tm=512, tf=1024):
    t = h.shape[0]
    tm = min([tm] + list(out_rows))
    d = D_MODEL
    outs = tuple(jax.ShapeDtypeStruct((r, d), F32) for r in out_rows)
    o_specs, bounds = _row_specs(outs, tm, d)
    kern = functools.partial(_mlp_kernel, bounds=bounds, final=final)
    vmem = (4 * tm * d * 4 + tm * d * 2 + tm * d * 4 + 4 * d * tf * 2 + 3 * tm * tf * 4
            + 2 * (len(outs) - 1) * tm * d * 4)
    return pl.pallas_call(
        kern,
        out_shape=outs,
        grid=(t // tm, D_FF // tf),
        in_specs=[
            pl.BlockSpec((tm, d), lambda i, f: (i, 0)),
            pl.BlockSpec((1, d), lambda i, f: (0, 0)),
            pl.BlockSpec((d, tf), lambda i, f: (0, f)),
            pl.BlockSpec((tf, d), lambda i, f: (f, 0)),
            pl.BlockSpec((1, d), lambda i, f: (0, 0)),
        ],
        out_specs=tuple(o_specs),
        scratch_shapes=[pltpu.VMEM((tm, d), BF16), pltpu.VMEM((tm, d), F32)],
        compiler_params=pltpu.CompilerParams(
            dimension_semantics=("arbitrary", "arbitrary"), vmem_limit_bytes=_vmem_limit(vmem + (8 << 20))),
        name="mlp",
    )(h, norm_w.reshape(1, d), wu, wd, final_w.reshape(1, d))


def _pack_w_in(w):
    d = w.shape[0]
    a_qkv, a_z = w[:, :3072], w[:, 3072:4096]
    ab = w[:, 4096:4128]
    b3 = w[:, 4128:7200]
    c0 = 7200
    cqkv = [w[:, c0 + part * C_WIDTH + g * C_OUT:c0 + part * C_WIDTH + (g + 1) * C_OUT]
            for g in range(len(C_GROUPS)) for part in range(3)]
    gates = w[:, 11808:]
    zeros = lambda n: jnp.zeros((d, n), w.dtype)
    packed = jnp.concatenate(
        [a_qkv, b3, a_z, ab, zeros(P_CQKV - P_AB - 32)] + cqkv + [gates], axis=1)
    assert packed.shape[1] == P_WIDTH
    return packed.astype(BF16)


def _trunk(xs, geom, rel_bias, norm_mix, w_in, conv_a, a_log, dt_bias, head_norm, conv_b,
           w_br_a, w_br_b, w_br_c, w_out, norm_mlp, w_up, w_down, norm_final):
    depth = w_in.shape[0]
    rows = tuple(a.shape[0] for a in xs)
    t = sum(rows)
    hs = list(xs)
    for layer in range(depth):
        last = layer == depth - 1
        p, ab = _in_proj(hs, norm_mix[layer], _pack_w_in(w_in[layer]))
        qkv, gb = _gdn_pre(p, ab, conv_a[layer], a_log[layer], dt_bias[layer], geom)
        gbt3 = gb[:, :32].reshape(t // GDN_CHUNK, GDN_CHUNK, 32).transpose(0, 2, 1)
        o_f, o_b = _gdn_scan(qkv, gb, gbt3, geom)
        r1, r2 = _deinterleave(p)
        attn = [_attn_group(p.reshape(1, t, P_WIDTH), P_CQKV // C_OUT, rel_bias, 0, geom),
                _attn_group(r1, 0, rel_bias, 1, geom),
                _attn_group(r2, 0, rel_bias, 2, geom)]
        y = _mix_prep(p, o_f, o_b, attn, head_norm[layer], conv_b[layer], geom)
        h = _merge(y, p, hs, w_br_a[layer].astype(BF16), w_br_b[layer].astype(BF16),
                   w_br_c[layer].astype(BF16), w_out[layer].astype(BF16))
        hs = list(_mlp(h, norm_mlp[layer], w_up[layer].astype(BF16), w_down[layer].astype(BF16),
                       norm_final, final=last, out_rows=rows if last else (t,)))
    return hs


def kernel(x_prompt, x_sample, rel_bias, norm_mix, w_in, conv_a, a_log, dt_bias, head_norm, conv_b,
           w_br_a, w_br_b, w_br_c, w_out, norm_mlp, w_up, w_down, norm_final):
    bp, sp, d = x_prompt.shape
    bs, ss, _ = x_sample.shape
    tp = bp * sp
    geom = (tp, sp, ss)
    xs = (x_prompt.reshape(tp, d), x_sample.reshape(bs * ss, d))
    y_p, y_s = _trunk(xs, geom, rel_bias, norm_mix, w_in, conv_a, a_log, dt_bias, head_norm, conv_b,
                      w_br_a, w_br_b, w_br_c, w_out, norm_mlp, w_up, w_down, norm_final)
    return y_p.reshape(bp, sp, d), y_s.reshape(bs, ss, d)
```

```python
import functools
import math

import jax
import jax.numpy as jnp
import numpy as np
from jax import lax
from jax.experimental import pallas as pl
from jax.experimental.pallas import tpu as pltpu

F32 = jnp.float32
BF16 = jnp.bfloat16

LANE = 128
VMEM_BYTES_V7X = 64 * 1024 * 1024

D_MODEL = 2048
A_HEADS = 8
HEAD_DIM = 128
A_WIDTH = A_HEADS * HEAD_DIM
GDN_CHUNK = 64
B_WIDTH = 1024
C_GROUPS = ((128, 1), (512, 4), (2048, 16))
C_HPG = 4
C_HEADS = 12
C_WIDTH = C_HEADS * HEAD_DIM
C_OUT = C_HPG * HEAD_DIM
N_BUCKETS = 32
REL_MAX_DIST = 2048
RADIUS = 64
D_FF = 4 * D_MODEL
EPS = 1e-6
NEG_INF = -1e30

P_QKV = 0
P_B3 = 3072
P_Z = 6144
P_AB = 7168
P_CQKV = 7680
P_GATES = 12288
P_WIDTH = 18432
Y_WIDTH = A_WIDTH + B_WIDTH + C_OUT


def _vmem_limit(nbytes):
    return int(min(max(nbytes, 16 * 1024 * 1024), VMEM_BYTES_V7X - 8 * 1024 * 1024))


def _seq_bounds(t0, geom):
    tp, sp, ss = geom
    in_p = t0 < tp
    length = jnp.where(in_p, sp, ss)
    base = jnp.where(in_p, 0, tp)
    lo = base + ((t0 - base) // length) * length
    return lo, lo + length


def _rms(x, w):
    ms = jnp.mean(x * x, axis=-1, keepdims=True)
    return x * lax.rsqrt(ms + EPS) * w


def _sigmoid(x):
    return 1.0 / (1.0 + jnp.exp(-x))


def _dot(a, b):
    return jnp.dot(a, b, preferred_element_type=F32)


def _dot_nt(a, b):
    return lax.dot_general(a, b, (((1,), (1,)), ((), ())), preferred_element_type=F32)


def _dot_tn(a, b):
    return lax.dot_general(a, b, (((0,), (0,)), ((), ())), preferred_element_type=F32)


def _row_specs(srcs, tm, width, single_buffer=False):
    specs, bounds = [], [0]
    for a in srcs:
        n, off = a.shape[0] // tm, bounds[-1]
        assert a.shape[0] % tm == 0
        mode = dict(pipeline_mode=pl.Buffered(1)) if single_buffer and len(srcs) > 1 else {}
        specs.append(pl.BlockSpec((tm, width), lambda i, *_, off=off, n=n: (jnp.clip(i - off, 0, n - 1), 0), **mode))
        bounds.append(off + n)
    return specs, tuple(bounds)


def _on_owner(refs, bounds, fn, extra_cond=None):
    i = pl.program_id(0)
    for k, ref in enumerate(refs):
        cond = extra_cond
        if len(refs) > 1:
            own = (i >= bounds[k]) & (i < bounds[k + 1])
            cond = own if cond is None else cond & own
        if cond is None:
            fn(ref)
        else:
            pl.when(cond)(functools.partial(fn, ref))


def _inproj_kernel(*refs, bounds, ab_tile, ab_off):
    nsrc = len(bounds) - 1
    x_refs = refs[:nsrc]
    nw_ref, w_ref, p_ref, ab_ref, xn_ref = refs[nsrc:]
    j = pl.program_id(1)

    def norm(x_ref):
        xn_ref[...] = _rms(x_ref[...], nw_ref[...]).astype(BF16)

    _on_owner(x_refs, bounds, norm, extra_cond=(j == 0))
    acc = _dot(xn_ref[...], w_ref[...])
    p_ref[...] = acc.astype(BF16)

    @pl.when(j == ab_tile)
    def _():
        ab_ref[...] = acc[:, ab_off:ab_off + LANE]


def _in_proj(hs, norm_w, w_packed, layer, *, tm=1024, tn=1024):
    t = sum(a.shape[0] for a in hs)
    tm = min([tm] + [a.shape[0] for a in hs])
    assert P_WIDTH % tn == 0
    x_specs, bounds = _row_specs(hs, tm, D_MODEL, single_buffer=True)
    kern = functools.partial(_inproj_kernel, bounds=bounds, ab_tile=P_AB // tn, ab_off=P_AB % tn)
    vmem = 2 * tm * D_MODEL * 4 + tm * D_MODEL * 2 + 2 * D_MODEL * tn * 2 + 2 * tm * tn * 2 + 2 * tm * tn * 4
    return pl.pallas_call(
        kern,
        out_shape=(jax.ShapeDtypeStruct((t, P_WIDTH), BF16), jax.ShapeDtypeStruct((t, LANE), F32)),
        grid=(t // tm, P_WIDTH // tn),
        in_specs=x_specs + [
            pl.BlockSpec((1, D_MODEL), lambda i, j: (0, 0)),
            pl.BlockSpec((None, D_MODEL, tn), lambda i, j: (layer, 0, j)),
        ],
        out_specs=(
            pl.BlockSpec((tm, tn), lambda i, j: (i, j)),
            pl.BlockSpec((tm, LANE), lambda i, j: (i, 0)),
        ),
        scratch_shapes=[pltpu.VMEM((tm, D_MODEL), BF16)],
        compiler_params=pltpu.CompilerParams(
            dimension_semantics=("parallel", "arbitrary"), vmem_limit_bytes=_vmem_limit(vmem + (8 << 20))),
        name="in_proj",
    )(*hs, norm_w.reshape(1, D_MODEL), w_packed)


def _conv3(x, prev_row, next_row, w_ref):
    tm = x.shape[0]
    rows = lax.broadcasted_iota(jnp.int32, (tm, 1), 0)
    x_m1 = jnp.where(rows == 0, prev_row, pltpu.roll(x, 1, 0))
    x_p1 = jnp.where(rows == tm - 1, next_row, pltpu.roll(x, tm - 1, 0))
    return x_m1 * w_ref[0:1, :] + x * w_ref[1:2, :] + x_p1 * w_ref[2:3, :]


def _halo_flags(tm, geom):
    t0 = pl.program_id(0) * tm
    lo, hi = _seq_bounds(t0, geom)
    return (t0 > lo).astype(F32), (t0 + tm < hi).astype(F32)


def _gdn_pre_kernel(x_ref, xp_ref, xn_ref, cw_ref, ab_ref, nalog_ref, dtb_ref, o_ref, gb_ref, *, tm, geom, halo):
    has_prev, has_next = _halo_flags(tm, geom)
    x = x_ref[...].astype(F32)
    prev_row = xp_ref[halo - 1:halo, :].astype(F32) * has_prev
    next_row = xn_ref[0:1, :].astype(F32) * has_next
    y = _conv3(x, prev_row, next_row, cw_ref)
    y = y * _sigmoid(y)
    for hd in range(2 * A_HEADS):
        sl = slice(hd * HEAD_DIM, (hd + 1) * HEAD_DIM)
        yh = y[:, sl]
        inv = lax.rsqrt(jnp.sum(yh * yh, axis=-1, keepdims=True) + EPS)
        if hd < A_HEADS:
            inv = inv * (HEAD_DIM ** -0.5)
        o_ref[:, sl] = (yh * inv).astype(BF16)
    o_ref[:, 2 * A_WIDTH:] = y[:, 2 * A_WIDTH:].astype(BF16)

    ab = ab_ref[...]
    xs = ab + dtb_ref[...]
    softplus = jnp.maximum(xs, 0.0) + jnp.log(1.0 + jnp.exp(-jnp.abs(xs)))
    g = nalog_ref[...] * softplus
    lane = lax.broadcasted_iota(jnp.int32, ab.shape, 1)
    gb_ref[...] = jnp.where(lane < 2 * A_HEADS, g, _sigmoid(ab))


def _gdn_pre(p, ab, conv_w, a_log, dt_bias, geom, *, tm=256, halo=16):
    t = p.shape[0]
    tm = min(tm, t)
    w = 3 * A_WIDTH
    nb = tm // halo
    last = t // halo - 1
    nalog = jnp.zeros((1, LANE), F32).at[0, :2 * A_HEADS].set(-jnp.exp(a_log.reshape(-1)))
    dtb = jnp.zeros((1, LANE), F32).at[0, :2 * A_HEADS].set(dt_bias.reshape(-1))
    kern = functools.partial(_gdn_pre_kernel, tm=tm, geom=geom, halo=halo)
    vmem = 2 * (tm * w * 2 * 2 + 2 * halo * w * 2) + 10 * tm * w * 4
    return pl.pallas_call(
        kern,
        out_shape=(jax.ShapeDtypeStruct((t, w), BF16), jax.ShapeDtypeStruct((t, LANE), F32)),
        grid=(t // tm,),
        in_specs=[
            pl.BlockSpec((tm, w), lambda i: (i, P_QKV // w)),
            pl.BlockSpec((halo, w), lambda i: (jnp.maximum(i * nb - 1, 0), P_QKV // w)),
            pl.BlockSpec((halo, w), lambda i: (jnp.minimum((i + 1) * nb, last), P_QKV // w)),
            pl.BlockSpec((3, w), lambda i: (0, 0)),
            pl.BlockSpec((tm, LANE), lambda i: (i, 0)),
            pl.BlockSpec((1, LANE), lambda i: (0, 0)),
            pl.BlockSpec((1, LANE), lambda i: (0, 0)),
        ],
        out_specs=(
            pl.BlockSpec((tm, w), lambda i: (i, 0)),
            pl.BlockSpec((tm, LANE), lambda i: (i, 0)),
        ),
        compiler_params=pltpu.CompilerParams(
            dimension_semantics=("parallel",), vmem_limit_bytes=_vmem_limit(vmem)),
        name="gdn_pre",
    )(p, p, p, conv_w, ab, nalog, dtb)


def _split3(x):
    x1 = x.astype(BF16)
    r1 = x - x1.astype(F32)
    x2 = r1.astype(BF16)
    r2 = r1 - x2.astype(F32)
    return x1, x2, r2.astype(BF16)


def _gdn_scan_kernel(qf_ref, kf_ref, vf_ref, gf_ref, grf_ref, qb_ref, kb_ref, vb_ref, gbk_ref, grb_ref,
                     of_ref, ob_ref, s_ref, wq_s, u_s, lhs2_s, egl_s, *, rb, nblk, geom):
    c = GDN_CHUNK
    nc = rb // c
    i = pl.program_id(0)
    t0f = i * rb
    t0b = (nblk - 1 - i) * rb
    lo_f, _ = _seq_bounds(t0f, geom)
    _, hi_b = _seq_bounds(t0b, geom)

    @pl.when(t0f == lo_f)
    def _():
        s_ref[0:A_HEADS] = jnp.zeros((A_HEADS, HEAD_DIM, HEAD_DIM), F32)

    @pl.when(t0b + rb == hi_b)
    def _():
        s_ref[A_HEADS:2 * A_HEADS] = jnp.zeros((A_HEADS, HEAD_DIM, HEAD_DIM), F32)

    ri = lax.broadcasted_iota(jnp.int32, (c, c), 0)
    ci = lax.broadcasted_iota(jnp.int32, (c, c), 1)
    lower_incl = ri >= ci
    upper_incl = ri <= ci
    tri_lo = jnp.where(lower_incl, 1.0, 0.0).astype(BF16)
    tri_up = jnp.where(upper_incl, 1.0, 0.0).astype(BF16)
    bd16 = (ri // 16) == (ci // 16)
    off32 = ((ri // 32) == (ci // 32)) & jnp.logical_not(bd16)
    off64 = (ri // 32) != (ci // 32)
    eye = jnp.where(ri == ci, 1.0, 0.0).astype(F32)
    bf = lambda x: x.astype(BF16)

    sides = (
        (0, qf_ref, kf_ref, vf_ref, gf_ref, grf_ref, tri_lo, tri_up, lower_incl, ri > ci),
        (1, qb_ref, kb_ref, vb_ref, gbk_ref, grb_ref, tri_up, tri_lo, upper_incl, ri < ci),
    )

    def prep(jp, carry):
        chains = []
        for direction, q_ref, k_ref, v_ref, g_ref, gr_ref, m_col, m_row, incl, strict in sides:
            for cidx in (2 * jp, 2 * jp + 1):
                r0 = pl.multiple_of(cidx * c, c)
                gcol = g_ref[pl.ds(r0, c), :]
                grow = gr_ref[cidx]
                g1, g2, g3 = _split3(gcol)
                gc_col = _dot(m_col, g1) + _dot(m_col, g2) + _dot(m_col, g3)
                h1, h2, h3 = _split3(grow)
                gc_row = _dot(h1, m_row) + _dot(h2, m_row) + _dot(h3, m_row)
                for hh in range(A_HEADS):
                    ch = direction * A_HEADS + hh
                    sl = slice(hh * HEAD_DIM, (hh + 1) * HEAD_DIM)
                    gcr = gc_row[ch:ch + 1, :]
                    chains.append(dict(
                        ch=ch, cidx=cidx, incl=incl, strict=strict,
                        q=q_ref[pl.ds(r0, c), sl], k=k_ref[pl.ds(r0, c), sl], v=v_ref[pl.ds(r0, c), sl],
                        gcr=gcr, beta_row=grow[2 * A_HEADS + ch:2 * A_HEADS + ch + 1, :],
                        gcc_b=jnp.broadcast_to(gc_col[:, ch:ch + 1], (c, HEAD_DIM)),
                        g_last=gcr[:, c - 1:c] if direction == 0 else gcr[:, 0:1]))

        gram = [_dot_nt(jnp.concatenate([x["q"], x["k"]], axis=0), x["k"]) for x in chains]
        zs, e32, e64 = [], [], []
        for x, gm in zip(chains, gram):
            decay = jnp.exp(jnp.where(x["incl"], x["gcc_b"][:, :c] - x["gcr"], NEG_INF))
            dec_beta = decay * x["beta_row"]
            k_t = jnp.transpose(x["k"].astype(F32))
            k_dec_t = k_t * (jnp.exp(x["g_last"] - x["gcr"]) * x["beta_row"])
            lhs2_s[x["cidx"], x["ch"]] = jnp.concatenate([bf(gm[:c] * dec_beta), bf(k_dec_t)], axis=0)
            egl_s[x["cidx"], x["ch"]] = jnp.broadcast_to(jnp.exp(x["g_last"]), (1, HEAD_DIM))
            l_mat = jnp.where(x["strict"], gm[c:] * dec_beta, 0.0)
            zs.append(jnp.where(bd16, -l_mat, 0.0))
            e32.append(bf(jnp.where(off32, l_mat, 0.0)))
            e64.append(bf(jnp.where(off64, l_mat, 0.0)))

        zb = [bf(z) for z in zs]
        xs = [eye + z for z in zs]
        pw = [bf(_dot(b, b)) for b in zb]
        for step in range(3):
            nxt = [bf(_dot(p, p)) for p in pw] if step < 2 else None
            xs = [x + _dot(bf(x), p) for x, p in zip(xs, pw)]
            pw = nxt
        for e in (e32, e64):
            xb = [bf(x) for x in xs]
            ts = [bf(_dot(b, a)) for b, a in zip(xb, e)]
            xs = [x - _dot(t, b) for x, t, b in zip(xs, ts, xb)]

        for x, t_inv in zip(chains, xs):
            eg = jnp.exp(x["gcc_b"])
            rhs = jnp.concatenate([x["v"], bf(x["k"].astype(F32) * eg)], axis=1)
            uw = _dot(bf(t_inv), rhs)
            u_s[x["cidx"], x["ch"]] = uw[:, :HEAD_DIM]
            wq_s[x["cidx"], x["ch"]] = jnp.concatenate(
                [bf(uw[:, HEAD_DIM:]), bf(x["q"].astype(F32) * eg)], axis=0)
        return carry

    lax.fori_loop(0, nc // 2, prep, 0)

    def scan(j, carry):
        chains = [(d * A_HEADS + hh, cidx, o_ref, slice(hh * HEAD_DIM, (hh + 1) * HEAD_DIM))
                  for d, cidx, o_ref in ((0, j, of_ref), (1, nc - 1 - j, ob_ref)) for hh in range(A_HEADS)]
        states = [s_ref[ch] for ch, _, _, _ in chains]
        ws_qs = [_dot(wq_s[cidx, ch], bf(s)) for (ch, cidx, _, _), s in zip(chains, states)]
        v_new = [bf(u_s[cidx, ch] - r[:c]) for (ch, cidx, _, _), r in zip(chains, ws_qs)]
        upd = [_dot(lhs2_s[cidx, ch], vn) for (ch, cidx, _, _), vn in zip(chains, v_new)]
        for (ch, cidx, o_ref, sl), s, r, r2 in zip(chains, states, ws_qs, upd):
            r0 = pl.multiple_of(cidx * c, c)
            o_ref[pl.ds(r0, c), sl] = (r[c:] + r2[:c]).astype(o_ref.dtype)
            s_ref[ch] = s * egl_s[cidx, ch] + r2[c:]
        return carry

    lax.fori_loop(0, nc, scan, 0)


def _gdn_scan(qkv, gb, gbt3, geom, *, rb=512):
    t = qkv.shape[0]
    rb = min(rb, t)
    nblk = t // rb
    ncb = rb // GDN_CHUNK
    kern = functools.partial(_gdn_scan_kernel, rb=rb, nblk=nblk, geom=geom)

    def fwd(col):
        return lambda i: (i, col)

    def bwd(col):
        return lambda i: (nblk - 1 - i, col)

    in_specs = []
    for mk in (fwd, bwd):
        in_specs += [
            pl.BlockSpec((rb, A_WIDTH), mk(0)),
            pl.BlockSpec((rb, A_WIDTH), mk(1)),
            pl.BlockSpec((rb, A_WIDTH), mk(2)),
            pl.BlockSpec((rb, LANE), mk(0)),
            pl.BlockSpec((ncb, 32, GDN_CHUNK), (lambda i: (i, 0, 0)) if mk is fwd else (lambda i: (nblk - 1 - i, 0, 0))),
        ]
    nch = 2 * A_HEADS
    assert ncb % 2 == 0
    vmem = (2 * (6 * rb * A_WIDTH * 2 + 2 * rb * LANE * 4 + 2 * rb * A_WIDTH * 2) + nch * HEAD_DIM * HEAD_DIM * 4
            + ncb * nch * HEAD_DIM * (2 * GDN_CHUNK * 2 + GDN_CHUNK * 4 + (GDN_CHUNK + HEAD_DIM) * 2 + 8 * 4))
    return pl.pallas_call(
        kern,
        out_shape=(jax.ShapeDtypeStruct((t, A_WIDTH), BF16), jax.ShapeDtypeStruct((t, A_WIDTH), BF16)),
        grid=(nblk,),
        in_specs=in_specs,
        out_specs=(pl.BlockSpec((rb, A_WIDTH), fwd(0)), pl.BlockSpec((rb, A_WIDTH), bwd(0))),
        scratch_shapes=[
            pltpu.VMEM((nch, HEAD_DIM, HEAD_DIM), F32),
            pltpu.VMEM((ncb, nch, 2 * GDN_CHUNK, HEAD_DIM), BF16),
            pltpu.VMEM((ncb, nch, GDN_CHUNK, HEAD_DIM), F32),
            pltpu.VMEM((ncb, nch, GDN_CHUNK + HEAD_DIM, GDN_CHUNK), BF16),
            pltpu.VMEM((ncb, nch, 1, HEAD_DIM), F32),
        ],
        compiler_params=pltpu.CompilerParams(
            dimension_semantics=("arbitrary",), vmem_limit_bytes=_vmem_limit(vmem + (16 << 20))),
        name="gdn_scan",
    )(qkv, qkv, qkv, gb, gbt3, qkv, qkv, qkv, gb, gbt3)


def _t5_bucket(rel):
    half = N_BUCKETS // 2
    exact = half // 2
    ret = np.where(rel > 0, half, 0)
    n = np.abs(rel)
    large = exact + (np.log(np.maximum(n, 1) / exact) / math.log(REL_MAX_DIST / exact) * (half - exact)).astype(np.int32)
    large = np.minimum(large, half - 1)
    return (ret + np.where(n < exact, n, large)).astype(np.int32)


def _attn_kernel(tab_ref, bkt_ref, q_ref, kp_ref, kc_ref, kn_ref, vp_ref, vc_ref, vn_ref,
                 o_ref, lse_ref, bias_ref, *, tq, dil, geom):
    r = pl.program_id(0)
    qb = pl.program_id(1)
    nk = tq + 2 * RADIUS

    @pl.when((r == 0) & (qb == 0))
    def _():
        bkt = bkt_ref[...]
        for hh in range(C_HPG):
            acc = jnp.full((tq, nk), NEG_INF, F32)
            for b in range(N_BUCKETS):
                acc = jnp.where(bkt == b, tab_ref[b * C_HPG + hh], acc)
            bias_ref[hh] = acc

    lo, hi = _seq_bounds(qb * tq * dil, geom)
    krow = qb * tq - RADIUS + lax.broadcasted_iota(jnp.int32, (1, nk), 1)
    in_seq = (krow * dil >= lo) & (krow * dil < hi)
    scale = HEAD_DIM ** -0.5
    heads = [slice(hh * HEAD_DIM, (hh + 1) * HEAD_DIM) for hh in range(C_HPG)]

    def window(p_ref, c_ref, n_ref, sl):
        return jnp.concatenate([p_ref[tq - RADIUS:, sl], c_ref[:, sl], n_ref[:RADIUS, sl]], axis=0)

    scores = [_dot_nt(q_ref[:, sl], window(kp_ref, kc_ref, kn_ref, sl)) for sl in heads]
    probs, dens = [], []
    for hh, sl in enumerate(heads):
        sc = jnp.where(in_seq, scores[hh] * scale + bias_ref[hh], NEG_INF)
        m = jnp.max(sc, axis=-1, keepdims=True)
        p = jnp.exp(sc - m)
        den = jnp.sum(p, axis=-1, keepdims=True)
        lse_ref[:, sl] = jnp.broadcast_to(m + jnp.log(den), (tq, HEAD_DIM))
        probs.append(p.astype(BF16))
        dens.append(den)
    for hh, sl in enumerate(heads):
        o = _dot(probs[hh], window(vp_ref, vc_ref, vn_ref, sl)) / dens[hh]
        o_ref[:, sl] = o.astype(o_ref.dtype)


def _attn_group(src, col0, rel_bias, gi, geom, *, tq=256):
    dil, rows, _ = src.shape
    assert dil == C_GROUPS[gi][1] and C_GROUPS[gi][0] // (2 * dil) == RADIUS
    tq = min(tq, min(geom[1], geom[2]) // dil)
    assert tq >= RADIUS and rows % tq == 0 and (geom[1] // dil) % tq == 0 and (geom[2] // dil) % tq == 0
    nq = rows // tq
    nk = tq + 2 * RADIUS
    rel = np.arange(nk)[None, :] - RADIUS - np.arange(tq)[:, None]
    bkt = np.where(np.abs(rel) <= RADIUS, _t5_bucket(rel * dil), -1).astype(np.int32)
    tab = rel_bias[:, gi * C_HPG:(gi + 1) * C_HPG].astype(F32).reshape(-1)
    cw = C_OUT

    def cur(col):
        return lambda r, qb: (r, qb, col)

    def prev(col):
        return lambda r, qb: (r, jnp.maximum(qb - 1, 0), col)

    def nxt(col):
        return lambda r, qb: (r, jnp.minimum(qb + 1, nq - 1), col)

    blk = (None, tq, cw)
    qcol, kcol, vcol = col0, col0 + 1, col0 + 2
    kern = functools.partial(_attn_kernel, tq=tq, dil=dil, geom=geom)
    return pl.pallas_call(
        kern,
        out_shape=(jax.ShapeDtypeStruct((dil, rows, cw), BF16), jax.ShapeDtypeStruct((dil, rows, cw), F32)),
        grid=(dil, nq),
        in_specs=[
            pl.BlockSpec(memory_space=pltpu.SMEM),
            pl.BlockSpec((tq, nk), lambda r, qb: (0, 0)),
            pl.BlockSpec(blk, cur(qcol)),
            pl.BlockSpec(blk, prev(kcol)), pl.BlockSpec(blk, cur(kcol)), pl.BlockSpec(blk, nxt(kcol)),
            pl.BlockSpec(blk, prev(vcol)), pl.BlockSpec(blk, cur(vcol)), pl.BlockSpec(blk, nxt(vcol)),
        ],
        out_specs=(pl.BlockSpec(blk, cur(0)), pl.BlockSpec(blk, cur(0))),
        scratch_shapes=[pltpu.VMEM((C_HPG, tq, nk), F32)],
        compiler_params=pltpu.CompilerParams(dimension_semantics=("arbitrary", "arbitrary")),
        name=f"attn_g{gi}",
    )(tab, jnp.asarray(bkt), src, src, src, src, src, src, src)


def _deint_kernel(x1_ref, x2_ref, o1_ref, o2_ref, scr_ref, *, tm):
    for x_ref, o_ref in ((x1_ref, o1_ref), (x2_ref, o2_ref)):
        dil = o_ref.shape[0]
        for cb in range(scr_ref.shape[0]):
            cols = slice(cb * LANE, (cb + 1) * LANE)
            scr_ref[cb] = x_ref[:, cols].astype(F32)
            for r in range(dil):
                o_ref[r, :, cols] = scr_ref[cb, pl.ds(r, tm // dil, stride=dil), :].astype(o_ref.dtype)


def _deinterleave(p, *, tm=512):
    t = p.shape[0]
    tm = min(tm, t)
    w = 3 * C_OUT
    d1, d2 = C_GROUPS[1][1], C_GROUPS[2][1]
    kern = functools.partial(_deint_kernel, tm=tm)
    return pl.pallas_call(
        kern,
        out_shape=(jax.ShapeDtypeStruct((d1, t // d1, w), BF16), jax.ShapeDtypeStruct((d2, t // d2, w), BF16)),
        grid=(t // tm,),
        in_specs=[pl.BlockSpec((tm, w), lambda i: (i, P_CQKV // w + 1)),
                  pl.BlockSpec((tm, w), lambda i: (i, P_CQKV // w + 2))],
        out_specs=(pl.BlockSpec((d1, tm // d1, w), lambda i: (0, i, 0)),
                   pl.BlockSpec((d2, tm // d2, w), lambda i: (0, i, 0))),
        scratch_shapes=[pltpu.VMEM((w // LANE, tm, LANE), F32)],
        compiler_params=pltpu.CompilerParams(dimension_semantics=("parallel",)),
        name="deinterleave",
    )(p, p)


def _mix_prep_kernel(of_ref, ob_ref, z_ref, hn_ref, b_ref, bp_ref, bn_ref, cw_ref,
                     o0_ref, o1_ref, o2_ref, l0_ref, l1_ref, l2_ref, y_ref, so1, so2, sl1, sl2, *, tm, geom, halo):
    o = of_ref[...].astype(F32) + ob_ref[...].astype(F32)
    z = z_ref[...].astype(F32)
    for hd in range(A_HEADS):
        sl = slice(hd * HEAD_DIM, (hd + 1) * HEAD_DIM)
        oh = o[:, sl]
        zh = z[:, sl]
        y = _rms(oh, hn_ref[...]) * (zh * _sigmoid(zh))
        y_ref[:, sl] = y.astype(BF16)

    has_prev, has_next = _halo_flags(tm, geom)

    def gated(ref, rows):
        return ref[rows, B_WIDTH:2 * B_WIDTH].astype(F32) * ref[rows, 2 * B_WIDTH:].astype(F32)

    u = gated(b_ref, slice(None))
    prev_row = gated(bp_ref, slice(halo - 1, halo)) * has_prev
    next_row = gated(bn_ref, slice(0, 1)) * has_next
    yb = b_ref[:, :B_WIDTH].astype(F32) * _conv3(u, prev_row, next_row, cw_ref)
    y_ref[:, A_WIDTH:A_WIDTH + B_WIDTH] = yb.astype(BF16)

    for src_ref, dst_ref in ((o1_ref, so1), (o2_ref, so2), (l1_ref, sl1), (l2_ref, sl2)):
        dil = src_ref.shape[0]
        for hd in range(C_HPG):
            for r in range(dil):
                dst_ref[hd, pl.ds(r, tm // dil, stride=dil), :] = (
                    src_ref[r, :, hd * HEAD_DIM:(hd + 1) * HEAD_DIM].astype(F32))
    for hd in range(C_HPG):
        sl = slice(hd * HEAD_DIM, (hd + 1) * HEAD_DIM)
        l0, l1, l2 = l0_ref[:, sl], sl1[hd], sl2[hd]
        m = jnp.maximum(jnp.maximum(l0, l1), l2)
        e0, e1, e2 = jnp.exp(l0 - m), jnp.exp(l1 - m), jnp.exp(l2 - m)
        num = e0 * o0_ref[:, sl].astype(F32) + e1 * so1[hd] + e2 * so2[hd]
        y_ref[:, A_WIDTH + B_WIDTH + hd * HEAD_DIM:A_WIDTH + B_WIDTH + (hd + 1) * HEAD_DIM] = (
            num / (e0 + e1 + e2)).astype(BF16)


def _mix_prep(p, o_f, o_b, attn, head_norm, conv_b, geom, *, tm=256, halo=16):
    t = p.shape[0]
    tm = min(tm, t)
    nb = tm // halo
    last = t // halo - 1
    w3 = 3 * B_WIDTH
    row = lambda w, col=0: pl.BlockSpec((tm, w), lambda i: (i, col))
    kern = functools.partial(_mix_prep_kernel, tm=tm, geom=geom, halo=halo)
    (o0, l0), (o1, l1), (o2, l2) = attn

    def res(a):
        dil = a.shape[0]
        if dil == 1:
            return pl.BlockSpec((None, tm, C_OUT), lambda i: (0, i, 0))
        return pl.BlockSpec((dil, tm // dil, C_OUT), lambda i: (0, i, 0))

    vmem = 2 * tm * (2 * A_WIDTH * 2 + A_WIDTH * 2 + w3 * 2 + 3 * C_OUT * 6 + Y_WIDTH * 2) + 12 * tm * A_WIDTH * 4
    return pl.pallas_call(
        kern,
        out_shape=jax.ShapeDtypeStruct((t, Y_WIDTH), BF16),
        grid=(t // tm,),
        in_specs=[
            row(A_WIDTH), row(A_WIDTH), row(A_WIDTH, P_Z // A_WIDTH),
            pl.BlockSpec((1, HEAD_DIM), lambda i: (0, 0)),
            row(w3, P_B3 // w3),
            pl.BlockSpec((halo, w3), lambda i: (jnp.maximum(i * nb - 1, 0), P_B3 // w3)),
            pl.BlockSpec((halo, w3), lambda i: (jnp.minimum((i + 1) * nb, last), P_B3 // w3)),
            pl.BlockSpec((3, B_WIDTH), lambda i: (0, 0)),
            res(o0), res(o1), res(o2), res(l0), res(l1), res(l2),
        ],
        out_specs=row(Y_WIDTH),
        scratch_shapes=[pltpu.VMEM((C_HPG, tm, HEAD_DIM), F32)] * 4,
        compiler_params=pltpu.CompilerParams(
            dimension_semantics=("parallel",), vmem_limit_bytes=_vmem_limit(vmem)),
        name="mix_prep",
    )(o_f, o_b, p, head_norm.reshape(1, HEAD_DIM), p, p, p, conv_b, o0, o1, o2, l0, l1, l2)


def _merge_kernel(*refs, bounds):
    nsrc = len(bounds) - 1
    h_refs = refs[:nsrc]
    y_ref, g_ref, wa_ref, wb_ref, wc_ref, wo_ref, o_ref = refs[nsrc:]
    d = D_MODEL
    merged = _sigmoid(g_ref[:, 0:d].astype(F32)) * _dot(y_ref[:, :A_WIDTH], wa_ref[...])
    merged += _sigmoid(g_ref[:, d:2 * d].astype(F32)) * _dot(y_ref[:, A_WIDTH:A_WIDTH + B_WIDTH], wb_ref[...])
    merged += _sigmoid(g_ref[:, 2 * d:].astype(F32)) * _dot(y_ref[:, A_WIDTH + B_WIDTH:], wc_ref[...])
    delta = _dot(merged.astype(BF16), wo_ref[...])

    def residual(h_ref):
        o_ref[...] = h_ref[...] + delta

    _on_owner(h_refs, bounds, residual)


def _merge(y, p, hs, wa, wb, wc, wo, layer, *, tm=256):
    t = y.shape[0]
    tm = min([tm] + [a.shape[0] for a in hs])
    d = D_MODEL
    h_specs, bounds = _row_specs(hs, tm, d)
    const = lambda shape: pl.BlockSpec((None,) + shape, lambda i: (layer, 0, 0), pipeline_mode=pl.Buffered(1))
    wbytes = (A_WIDTH + B_WIDTH + C_OUT + d) * d * 2
    vmem = wbytes + 2 * tm * (Y_WIDTH * 2 + 3 * d * 2 + len(hs) * d * 4 + d * 4) + 6 * tm * d * 4
    return pl.pallas_call(
        functools.partial(_merge_kernel, bounds=bounds),
        out_shape=jax.ShapeDtypeStruct((t, d), F32),
        grid=(t // tm,),
        in_specs=h_specs + [
            pl.BlockSpec((tm, Y_WIDTH), lambda i: (i, 0)),
            pl.BlockSpec((tm, 3 * d), lambda i: (i, P_GATES // (3 * d))),
            const((A_WIDTH, d)), const((B_WIDTH, d)), const((C_OUT, d)), const((d, d)),
        ],
        out_specs=pl.BlockSpec((tm, d), lambda i: (i, 0)),
        compiler_params=pltpu.CompilerParams(
            dimension_semantics=("parallel",), vmem_limit_bytes=_vmem_limit(vmem)),
        name="merge",
    )(*hs, y, p, wa, wb, wc, wo)


def _mlp_kernel(h_ref, nw_ref, wu_ref, wd_ref, fw_ref, *refs, bounds, final):
    nout = len(bounds) - 1
    o_refs = refs[:nout]
    xn_ref, acc_ref = refs[nout:]
    f = pl.program_id(1)

    @pl.when(f == 0)
    def _():
        xn_ref[...] = _rms(h_ref[...], nw_ref[...]).astype(BF16)
        acc_ref[...] = jnp.zeros_like(acc_ref)

    up = jnp.maximum(_dot(xn_ref[...], wu_ref[...]), 0.0)
    acc_ref[...] += _dot((up * up).astype(BF16), wd_ref[...])

    def emit(o_ref):
        hn = h_ref[...] + acc_ref[...]
        o_ref[...] = _rms(hn, fw_ref[...]) if final else hn

    _on_owner(o_refs, bounds, emit, extra_cond=(f == pl.num_programs(1) - 1))


def _mlp(h, norm_w, wu, wd, final_w, layer, *, final, out_rows, tm=512, tf=1024):
    t = h.shape[0]
    tm = min([tm] + list(out_rows))
    d = D_MODEL
    outs = tuple(jax.ShapeDtypeStruct((r, d), F32) for r in out_rows)
    o_specs, bounds = _row_specs(outs, tm, d)
    kern = functools.partial(_mlp_kernel, bounds=bounds, final=final)
    vmem = (4 * tm * d * 4 + tm * d * 2 + tm * d * 4 + 4 * d * tf * 2 + 3 * tm * tf * 4
            + 2 * (len(outs) - 1) * tm * d * 4)
    return pl.pallas_call(
        kern,
        out_shape=outs,
        grid=(t // tm, D_FF // tf),
        in_specs=[
            pl.BlockSpec((tm, d), lambda i, f: (i, 0)),
            pl.BlockSpec((1, d), lambda i, f: (0, 0)),
            pl.BlockSpec((None, d, tf), lambda i, f: (layer, 0, f)),
            pl.BlockSpec((None, tf, d), lambda i, f: (layer, f, 0)),
            pl.BlockSpec((1, d), lambda i, f: (0, 0)),
        ],
        out_specs=tuple(o_specs),
        scratch_shapes=[pltpu.VMEM((tm, d), BF16), pltpu.VMEM((tm, d), F32)],
        compiler_params=pltpu.CompilerParams(
            dimension_semantics=("arbitrary", "arbitrary"), vmem_limit_bytes=_vmem_limit(vmem + (8 << 20))),
        name="mlp",
    )(h, norm_w.reshape(1, d), wu, wd, final_w.reshape(1, d))


def _w_in_pieces():
    c0 = 7200
    pieces = [(0, P_QKV, 3072), (4128, P_B3, 3072), (3072, P_Z, 1024), (4096, P_AB, 32)]
    for g in range(len(C_GROUPS)):
        for part in range(3):
            pieces.append((c0 + part * C_WIDTH + g * C_OUT, P_CQKV + (3 * g + part) * C_OUT, C_OUT))
    pieces.append((11808, P_GATES, 3 * D_MODEL))
    return pieces


def _pack_kernel(w_ref, o_ref):
    for src, dst, width in _w_in_pieces():
        o_ref[:, dst:dst + width] = w_ref[:, src:src + width].astype(BF16)
    o_ref[:, P_AB + 32:P_CQKV] = jnp.zeros((o_ref.shape[0], P_CQKV - P_AB - 32), BF16)


def _pack_w_in(w, *, tr=128):
    nl, d, cols = w.shape
    return pl.pallas_call(
        _pack_kernel,
        out_shape=jax.ShapeDtypeStruct((nl, d, P_WIDTH), BF16),
        grid=(nl, d // tr),
        in_specs=[pl.BlockSpec((None, tr, cols), lambda l, i: (l, i, 0))],
        out_specs=pl.BlockSpec((None, tr, P_WIDTH), lambda l, i: (l, i, 0)),
        compiler_params=pltpu.CompilerParams(
            dimension_semantics=("parallel", "parallel"),
            vmem_limit_bytes=_vmem_limit(2 * tr * (cols * 4 + P_WIDTH * 2) + (16 << 20))),
        name="pack_w_in",
    )(w)


def _cast_kernel(w_ref, o_ref):
    o_ref[...] = w_ref[...].astype(BF16)


def _cast_bf16(w, *, block_bytes=4 << 20):
    nl, r, c = w.shape
    tr = max(8, min(r, block_bytes // (c * 4)))
    assert r % tr == 0
    return pl.pallas_call(
        _cast_kernel,
        out_shape=jax.ShapeDtypeStruct(w.shape, BF16),
        grid=(nl, r // tr),
        in_specs=[pl.BlockSpec((None, tr, c), lambda l, i: (l, i, 0))],
        out_specs=pl.BlockSpec((None, tr, c), lambda l, i: (l, i, 0)),
        compiler_params=pltpu.CompilerParams(dimension_semantics=("parallel", "parallel")),
        name="cast_bf16",
    )(w)


def _trunk(xs, geom, rel_bias, norm_mix, w_in, conv_a, a_log, dt_bias, head_norm, conv_b,
           w_br_a, w_br_b, w_br_c, w_out, norm_mlp, w_up, w_down, norm_final):
    depth = w_in.shape[0]
    rows = tuple(a.shape[0] for a in xs)
    t = sum(rows)
    hs = list(xs)
    w_in_b = _pack_w_in(w_in)
    wa_b, wb_b, wc_b, wo_b, wu_b, wd_b = (_cast_bf16(w) for w in (w_br_a, w_br_b, w_br_c, w_out, w_up, w_down))
    for layer in range(depth):
        last = layer == depth - 1
        p, ab = _in_proj(hs, norm_mix[layer], w_in_b, layer)
        qkv, gb = _gdn_pre(p, ab, conv_a[layer], a_log[layer], dt_bias[layer], geom)
        gbt3 = gb[:, :32].reshape(t // GDN_CHUNK, GDN_CHUNK, 32).transpose(0, 2, 1)
        o_f, o_b = _gdn_scan(qkv, gb, gbt3, geom)
        r1, r2 = _deinterleave(p)
        attn = [_attn_group(p.reshape(1, t, P_WIDTH), P_CQKV // C_OUT, rel_bias, 0, geom),
                _attn_group(r1, 0, rel_bias, 1, geom),
                _attn_group(r2, 0, rel_bias, 2, geom)]
        y = _mix_prep(p, o_f, o_b, attn, head_norm[layer], conv_b[layer], geom)
        h = _merge(y, p, hs, wa_b, wb_b, wc_b, wo_b, layer)
        hs = list(_mlp(h, norm_mlp[layer], wu_b, wd_b, norm_final, layer,
                       final=last, out_rows=rows if last else (t,)))
    return hs


def kernel(x_prompt, x_sample, rel_bias, norm_mix, w_in, conv_a, a_log, dt_bias, head_norm, conv_b,
           w_br_a, w_br_b, w_br_c, w_out, norm_mlp, w_up, w_down, norm_final):
    bp, sp, d = x_prompt.shape
    bs, ss, _ = x_sample.shape
    tp = bp * sp
    geom = (tp, sp, ss)
    xs = (x_prompt.reshape(tp, d), x_sample.reshape(bs * ss, d))
    y_p, y_s = _trunk(xs, geom, rel_bias, norm_mix, w_in, conv_a, a_log, dt_bias, head_norm, conv_b,
                      w_br_a, w_br_b, w_br_c, w_out, norm_mlp, w_up, w_down, norm_final)
    return y_p.reshape(bp, sp, d), y_s.reshape(bs, ss, d)
```

```python
import functools
import math

import jax
import jax.numpy as jnp
import numpy as np
from jax import lax
from jax.experimental import pallas as pl
from jax.experimental.pallas import tpu as pltpu

F32 = jnp.float32
BF16 = jnp.bfloat16

LANE = 128
VMEM_BYTES_V7X = 64 * 1024 * 1024

D_MODEL = 2048
A_HEADS = 8
HEAD_DIM = 128
A_WIDTH = A_HEADS * HEAD_DIM
GDN_CHUNK = 64
B_WIDTH = 1024
C_GROUPS = ((128, 1), (512, 4), (2048, 16))
C_HPG = 4
C_HEADS = 12
C_WIDTH = C_HEADS * HEAD_DIM
C_OUT = C_HPG * HEAD_DIM
N_BUCKETS = 32
REL_MAX_DIST = 2048
RADIUS = 64
D_FF = 4 * D_MODEL
EPS = 1e-6
NEG_INF = -1e30

P_QKV = 0
P_B3 = 3072
P_Z = 6144
P_AB = 7168
P_CQKV = 7680
P_GATES = 12288
P_WIDTH = 18432
Y_WIDTH = A_WIDTH + B_WIDTH + C_OUT


def _vmem_limit(nbytes):
    return int(min(max(nbytes, 16 * 1024 * 1024), VMEM_BYTES_V7X - 8 * 1024 * 1024))


def _seq_bounds(t0, geom):
    tp, sp, ss = geom
    in_p = t0 < tp
    length = jnp.where(in_p, sp, ss)
    base = jnp.where(in_p, 0, tp)
    lo = base + ((t0 - base) // length) * length
    return lo, lo + length


def _rms(x, w):
    ms = jnp.mean(x * x, axis=-1, keepdims=True)
    return x * lax.rsqrt(ms + EPS) * w


def _sigmoid(x):
    return 1.0 / (1.0 + jnp.exp(-x))


def _dot(a, b):
    return jnp.dot(a, b, preferred_element_type=F32)


def _dot_nt(a, b):
    return lax.dot_general(a, b, (((1,), (1,)), ((), ())), preferred_element_type=F32)


def _dot_tn(a, b):
    return lax.dot_general(a, b, (((0,), (0,)), ((), ())), preferred_element_type=F32)


def _row_specs(srcs, tm, width, single_buffer=False):
    specs, bounds = [], [0]
    for a in srcs:
        n, off = a.shape[0] // tm, bounds[-1]
        assert a.shape[0] % tm == 0
        mode = dict(pipeline_mode=pl.Buffered(1)) if single_buffer and len(srcs) > 1 else {}
        specs.append(pl.BlockSpec((tm, width), lambda i, *_, off=off, n=n: (jnp.clip(i - off, 0, n - 1), 0), **mode))
        bounds.append(off + n)
    return specs, tuple(bounds)


def _on_owner(refs, bounds, fn, extra_cond=None):
    i = pl.program_id(0)
    for k, ref in enumerate(refs):
        cond = extra_cond
        if len(refs) > 1:
            own = (i >= bounds[k]) & (i < bounds[k + 1])
            cond = own if cond is None else cond & own
        if cond is None:
            fn(ref)
        else:
            pl.when(cond)(functools.partial(fn, ref))


def _inproj_kernel(*refs, bounds, ab_tile, ab_off):
    nsrc = len(bounds) - 1
    x_refs = refs[:nsrc]
    nw_ref, w_ref, p_ref, ab_ref, xn_ref = refs[nsrc:]
    j = pl.program_id(1)

    def norm(x_ref):
        xn_ref[...] = _rms(x_ref[...], nw_ref[...]).astype(BF16)

    _on_owner(x_refs, bounds, norm, extra_cond=(j == 0))
    acc = _dot_nt(xn_ref[...], w_ref[...])
    p_ref[...] = acc.astype(BF16)

    @pl.when(j == ab_tile)
    def _():
        ab_ref[...] = acc[:, ab_off:ab_off + LANE]


def _in_proj(hs, norm_w, w_packed, layer, *, tm=1024, tn=1024):
    t = sum(a.shape[0] for a in hs)
    tm = min([tm] + [a.shape[0] for a in hs])
    assert P_WIDTH % tn == 0
    x_specs, bounds = _row_specs(hs, tm, D_MODEL, single_buffer=True)
    kern = functools.partial(_inproj_kernel, bounds=bounds, ab_tile=P_AB // tn, ab_off=P_AB % tn)
    vmem = 2 * tm * D_MODEL * 4 + tm * D_MODEL * 2 + 2 * D_MODEL * tn * 2 + 2 * tm * tn * 2 + 2 * tm * tn * 4
    return pl.pallas_call(
        kern,
        out_shape=(jax.ShapeDtypeStruct((t, P_WIDTH), BF16), jax.ShapeDtypeStruct((t, LANE), F32)),
        grid=(t // tm, P_WIDTH // tn),
        in_specs=x_specs + [
            pl.BlockSpec((1, D_MODEL), lambda i, j: (0, 0)),
            pl.BlockSpec((None, tn, D_MODEL), lambda i, j: (layer, j, 0)),
        ],
        out_specs=(
            pl.BlockSpec((tm, tn), lambda i, j: (i, j)),
            pl.BlockSpec((tm, LANE), lambda i, j: (i, 0)),
        ),
        scratch_shapes=[pltpu.VMEM((tm, D_MODEL), BF16)],
        compiler_params=pltpu.CompilerParams(
            dimension_semantics=("parallel", "arbitrary"), vmem_limit_bytes=_vmem_limit(vmem + (8 << 20))),
        name="in_proj",
    )(*hs, norm_w.reshape(1, D_MODEL), w_packed)


def _conv3(x, prev_row, next_row, w_ref):
    tm = x.shape[0]
    rows = lax.broadcasted_iota(jnp.int32, (tm, 1), 0)
    x_m1 = jnp.where(rows == 0, prev_row, pltpu.roll(x, 1, 0))
    x_p1 = jnp.where(rows == tm - 1, next_row, pltpu.roll(x, tm - 1, 0))
    return x_m1 * w_ref[0:1, :] + x * w_ref[1:2, :] + x_p1 * w_ref[2:3, :]


def _halo_flags(tm, geom):
    t0 = pl.program_id(0) * tm
    lo, hi = _seq_bounds(t0, geom)
    return (t0 > lo).astype(F32), (t0 + tm < hi).astype(F32)


def _gdn_pre_kernel(x_ref, xp_ref, xn_ref, cw_ref, ab_ref, nalog_ref, dtb_ref, o_ref, gb_ref, *, tm, geom, halo):
    has_prev, has_next = _halo_flags(tm, geom)
    x = x_ref[...].astype(F32)
    prev_row = xp_ref[halo - 1:halo, :].astype(F32) * has_prev
    next_row = xn_ref[0:1, :].astype(F32) * has_next
    y = _conv3(x, prev_row, next_row, cw_ref)
    y = y * _sigmoid(y)
    for hd in range(2 * A_HEADS):
        sl = slice(hd * HEAD_DIM, (hd + 1) * HEAD_DIM)
        yh = y[:, sl]
        inv = lax.rsqrt(jnp.sum(yh * yh, axis=-1, keepdims=True) + EPS)
        if hd < A_HEADS:
            inv = inv * (HEAD_DIM ** -0.5)
        o_ref[:, sl] = (yh * inv).astype(BF16)
    o_ref[:, 2 * A_WIDTH:] = y[:, 2 * A_WIDTH:].astype(BF16)

    ab = ab_ref[...]
    xs = ab + dtb_ref[...]
    softplus = jnp.maximum(xs, 0.0) + jnp.log(1.0 + jnp.exp(-jnp.abs(xs)))
    g = nalog_ref[...] * softplus
    lane = lax.broadcasted_iota(jnp.int32, ab.shape, 1)
    gb_ref[...] = jnp.where(lane < 2 * A_HEADS, g, _sigmoid(ab))


def _gdn_pre(p, ab, conv_w, a_log, dt_bias, geom, *, tm=256, halo=16):
    t = p.shape[0]
    tm = min(tm, t)
    w = 3 * A_WIDTH
    nb = tm // halo
    last = t // halo - 1
    nalog = jnp.zeros((1, LANE), F32).at[0, :2 * A_HEADS].set(-jnp.exp(a_log.reshape(-1)))
    dtb = jnp.zeros((1, LANE), F32).at[0, :2 * A_HEADS].set(dt_bias.reshape(-1))
    kern = functools.partial(_gdn_pre_kernel, tm=tm, geom=geom, halo=halo)
    vmem = 2 * (tm * w * 2 * 2 + 2 * halo * w * 2) + 10 * tm * w * 4
    return pl.pallas_call(
        kern,
        out_shape=(jax.ShapeDtypeStruct((t, w), BF16), jax.ShapeDtypeStruct((t, LANE), F32)),
        grid=(t // tm,),
        in_specs=[
            pl.BlockSpec((tm, w), lambda i: (i, P_QKV // w)),
            pl.BlockSpec((halo, w), lambda i: (jnp.maximum(i * nb - 1, 0), P_QKV // w)),
            pl.BlockSpec((halo, w), lambda i: (jnp.minimum((i + 1) * nb, last), P_QKV // w)),
            pl.BlockSpec((3, w), lambda i: (0, 0)),
            pl.BlockSpec((tm, LANE), lambda i: (i, 0)),
            pl.BlockSpec((1, LANE), lambda i: (0, 0)),
            pl.BlockSpec((1, LANE), lambda i: (0, 0)),
        ],
        out_specs=(
            pl.BlockSpec((tm, w), lambda i: (i, 0)),
            pl.BlockSpec((tm, LANE), lambda i: (i, 0)),
        ),
        compiler_params=pltpu.CompilerParams(
            dimension_semantics=("parallel",), vmem_limit_bytes=_vmem_limit(vmem)),
        name="gdn_pre",
    )(p, p, p, conv_w, ab, nalog, dtb)


def _split3(x):
    x1 = x.astype(BF16)
    r1 = x - x1.astype(F32)
    x2 = r1.astype(BF16)
    r2 = r1 - x2.astype(F32)
    return x1, x2, r2.astype(BF16)


def _gdn_scan_kernel(qf_ref, kf_ref, vf_ref, gf_ref, grf_ref, qb_ref, kb_ref, vb_ref, gbk_ref, grb_ref,
                     of_ref, ob_ref, s_ref, wq_s, u_s, lhs2_s, egl_s, *, rb, nblk, geom):
    c = GDN_CHUNK
    nc = rb // c
    i = pl.program_id(0)
    t0f = i * rb
    t0b = (nblk - 1 - i) * rb
    lo_f, _ = _seq_bounds(t0f, geom)
    _, hi_b = _seq_bounds(t0b, geom)

    @pl.when(t0f == lo_f)
    def _():
        s_ref[0:A_HEADS] = jnp.zeros((A_HEADS, HEAD_DIM, HEAD_DIM), F32)

    @pl.when(t0b + rb == hi_b)
    def _():
        s_ref[A_HEADS:2 * A_HEADS] = jnp.zeros((A_HEADS, HEAD_DIM, HEAD_DIM), F32)

    ri = lax.broadcasted_iota(jnp.int32, (c, c), 0)
    ci = lax.broadcasted_iota(jnp.int32, (c, c), 1)
    lower_incl = ri >= ci
    upper_incl = ri <= ci
    tri_lo = jnp.where(lower_incl, 1.0, 0.0).astype(BF16)
    tri_up = jnp.where(upper_incl, 1.0, 0.0).astype(BF16)
    bd16 = (ri // 16) == (ci // 16)
    off32 = ((ri // 32) == (ci // 32)) & jnp.logical_not(bd16)
    off64 = (ri // 32) != (ci // 32)
    eye = jnp.where(ri == ci, 1.0, 0.0).astype(F32)
    bf = lambda x: x.astype(BF16)

    sides = (
        (0, qf_ref, kf_ref, vf_ref, gf_ref, grf_ref, tri_lo, tri_up, lower_incl, ri > ci),
        (1, qb_ref, kb_ref, vb_ref, gbk_ref, grb_ref, tri_up, tri_lo, upper_incl, ri < ci),
    )

    def prep(jp, carry):
        chains = []
        for direction, q_ref, k_ref, v_ref, g_ref, gr_ref, m_col, m_row, incl, strict in sides:
            for cidx in (2 * jp, 2 * jp + 1):
                r0 = pl.multiple_of(cidx * c, c)
                gcol = g_ref[pl.ds(r0, c), :]
                grow = gr_ref[cidx]
                g1, g2, g3 = _split3(gcol)
                gc_col = _dot(m_col, g1) + _dot(m_col, g2) + _dot(m_col, g3)
                h1, h2, h3 = _split3(grow)
                gc_row = _dot(h1, m_row) + _dot(h2, m_row) + _dot(h3, m_row)
                for hh in range(A_HEADS):
                    ch = direction * A_HEADS + hh
                    sl = slice(hh * HEAD_DIM, (hh + 1) * HEAD_DIM)
                    gcr = gc_row[ch:ch + 1, :]
                    chains.append(dict(
                        ch=ch, cidx=cidx, incl=incl, strict=strict,
                        q=q_ref[pl.ds(r0, c), sl], k=k_ref[pl.ds(r0, c), sl], v=v_ref[pl.ds(r0, c), sl],
                        gcr=gcr, beta_row=grow[2 * A_HEADS + ch:2 * A_HEADS + ch + 1, :],
                        gcc_b=jnp.broadcast_to(gc_col[:, ch:ch + 1], (c, HEAD_DIM)),
                        g_last=gcr[:, c - 1:c] if direction == 0 else gcr[:, 0:1]))

        gram = [_dot_nt(jnp.concatenate([x["q"], x["k"]], axis=0), x["k"]) for x in chains]
        zs, e32, e64 = [], [], []
        for x, gm in zip(chains, gram):
            decay = jnp.exp(jnp.where(x["incl"], x["gcc_b"][:, :c] - x["gcr"], NEG_INF))
            dec_beta = decay * x["beta_row"]
            k_t = jnp.transpose(x["k"].astype(F32))
            k_dec_t = k_t * (jnp.exp(x["g_last"] - x["gcr"]) * x["beta_row"])
            lhs2_s[x["cidx"], x["ch"]] = jnp.concatenate([bf(gm[:c] * dec_beta), bf(k_dec_t)], axis=0)
            egl_s[x["cidx"], x["ch"]] = jnp.broadcast_to(jnp.exp(x["g_last"]), (1, HEAD_DIM))
            l_mat = jnp.where(x["strict"], gm[c:] * dec_beta, 0.0)
            zs.append(jnp.where(bd16, -l_mat, 0.0))
            e32.append(bf(jnp.where(off32, l_mat, 0.0)))
            e64.append(bf(jnp.where(off64, l_mat, 0.0)))

        zb = [bf(z) for z in zs]
        xs = [eye + z for z in zs]
        pw = [bf(_dot(b, b)) for b in zb]
        for step in range(3):
            nxt = [bf(_dot(p, p)) for p in pw] if step < 2 else None
            xs = [x + _dot(bf(x), p) for x, p in zip(xs, pw)]
            pw = nxt
        for e in (e32, e64):
            xb = [bf(x) for x in xs]
            ts = [bf(_dot(b, a)) for b, a in zip(xb, e)]
            xs = [x - _dot(t, b) for x, t, b in zip(xs, ts, xb)]

        for x, t_inv in zip(chains, xs):
            eg = jnp.exp(x["gcc_b"])
            rhs = jnp.concatenate([x["v"], bf(x["k"].astype(F32) * eg)], axis=1)
            uw = _dot(bf(t_inv), rhs)
            u_s[x["cidx"], x["ch"]] = uw[:, :HEAD_DIM]
            wq_s[x["cidx"], x["ch"]] = jnp.concatenate(
                [bf(uw[:, HEAD_DIM:]), bf(x["q"].astype(F32) * eg)], axis=0)
        return carry

    lax.fori_loop(0, nc // 2, prep, 0)

    def scan(j, carry):
        chains = [(d * A_HEADS + hh, cidx, o_ref, slice(hh * HEAD_DIM, (hh + 1) * HEAD_DIM))
                  for d, cidx, o_ref in ((0, j, of_ref), (1, nc - 1 - j, ob_ref)) for hh in range(A_HEADS)]
        states = [s_ref[ch] for ch, _, _, _ in chains]
        ws_qs = [_dot(wq_s[cidx, ch], bf(s)) for (ch, cidx, _, _), s in zip(chains, states)]
        v_new = [bf(u_s[cidx, ch] - r[:c]) for (ch, cidx, _, _), r in zip(chains, ws_qs)]
        upd = [_dot(lhs2_s[cidx, ch], vn) for (ch, cidx, _, _), vn in zip(chains, v_new)]
        for (ch, cidx, o_ref, sl), s, r, r2 in zip(chains, states, ws_qs, upd):
            r0 = pl.multiple_of(cidx * c, c)
            o_ref[pl.ds(r0, c), sl] = (r[c:] + r2[:c]).astype(o_ref.dtype)
            s_ref[ch] = s * egl_s[cidx, ch] + r2[c:]
        return carry

    lax.fori_loop(0, nc, scan, 0)


def _gdn_scan(qkv, gb, gbt3, geom, *, rb=512):
    t = qkv.shape[0]
    rb = min(rb, t)
    nblk = t // rb
    ncb = rb // GDN_CHUNK
    kern = functools.partial(_gdn_scan_kernel, rb=rb, nblk=nblk, geom=geom)

    def fwd(col):
        return lambda i: (i, col)

    def bwd(col):
        return lambda i: (nblk - 1 - i, col)

    in_specs = []
    for mk in (fwd, bwd):
        in_specs += [
            pl.BlockSpec((rb, A_WIDTH), mk(0)),
            pl.BlockSpec((rb, A_WIDTH), mk(1)),
            pl.BlockSpec((rb, A_WIDTH), mk(2)),
            pl.BlockSpec((rb, LANE), mk(0)),
            pl.BlockSpec((ncb, 32, GDN_CHUNK), (lambda i: (i, 0, 0)) if mk is fwd else (lambda i: (nblk - 1 - i, 0, 0))),
        ]
    nch = 2 * A_HEADS
    assert ncb % 2 == 0
    vmem = (2 * (6 * rb * A_WIDTH * 2 + 2 * rb * LANE * 4 + 2 * rb * A_WIDTH * 2) + nch * HEAD_DIM * HEAD_DIM * 4
            + ncb * nch * HEAD_DIM * (2 * GDN_CHUNK * 2 + GDN_CHUNK * 4 + (GDN_CHUNK + HEAD_DIM) * 2 + 8 * 4))
    return pl.pallas_call(
        kern,
        out_shape=(jax.ShapeDtypeStruct((t, A_WIDTH), BF16), jax.ShapeDtypeStruct((t, A_WIDTH), BF16)),
        grid=(nblk,),
        in_specs=in_specs,
        out_specs=(pl.BlockSpec((rb, A_WIDTH), fwd(0)), pl.BlockSpec((rb, A_WIDTH), bwd(0))),
        scratch_shapes=[
            pltpu.VMEM((nch, HEAD_DIM, HEAD_DIM), F32),
            pltpu.VMEM((ncb, nch, 2 * GDN_CHUNK, HEAD_DIM), BF16),
            pltpu.VMEM((ncb, nch, GDN_CHUNK, HEAD_DIM), F32),
            pltpu.VMEM((ncb, nch, GDN_CHUNK + HEAD_DIM, GDN_CHUNK), BF16),
            pltpu.VMEM((ncb, nch, 1, HEAD_DIM), F32),
        ],
        compiler_params=pltpu.CompilerParams(
            dimension_semantics=("arbitrary",), vmem_limit_bytes=_vmem_limit(vmem + (16 << 20))),
        name="gdn_scan",
    )(qkv, qkv, qkv, gb, gbt3, qkv, qkv, qkv, gb, gbt3)


def _t5_bucket(rel):
    half = N_BUCKETS // 2
    exact = half // 2
    ret = np.where(rel > 0, half, 0)
    n = np.abs(rel)
    large = exact + (np.log(np.maximum(n, 1) / exact) / math.log(REL_MAX_DIST / exact) * (half - exact)).astype(np.int32)
    large = np.minimum(large, half - 1)
    return (ret + np.where(n < exact, n, large)).astype(np.int32)


def _attn_kernel(tab_ref, bkt_ref, q_ref, kp_ref, kc_ref, kn_ref, vp_ref, vc_ref, vn_ref,
                 o_ref, lse_ref, bias_ref, *, tq, dil, geom):
    r = pl.program_id(0)
    qb = pl.program_id(1)
    nk = tq + 2 * RADIUS

    @pl.when((r == 0) & (qb == 0))
    def _():
        bkt = bkt_ref[...]
        for hh in range(C_HPG):
            acc = jnp.full((tq, nk), NEG_INF, F32)
            for b in range(N_BUCKETS):
                acc = jnp.where(bkt == b, tab_ref[b * C_HPG + hh], acc)
            bias_ref[hh] = acc

    lo, hi = _seq_bounds(qb * tq * dil, geom)
    krow = qb * tq - RADIUS + lax.broadcasted_iota(jnp.int32, (1, nk), 1)
    in_seq = (krow * dil >= lo) & (krow * dil < hi)
    scale = HEAD_DIM ** -0.5
    heads = [slice(hh * HEAD_DIM, (hh + 1) * HEAD_DIM) for hh in range(C_HPG)]

    def window(p_ref, c_ref, n_ref, sl):
        return jnp.concatenate([p_ref[:, sl], c_ref[:, sl], n_ref[:, sl]], axis=0)

    scores = [_dot_nt(q_ref[:, sl], window(kp_ref, kc_ref, kn_ref, sl)) for sl in heads]
    probs, dens = [], []
    for hh, sl in enumerate(heads):
        sc = jnp.where(in_seq, scores[hh] * scale + bias_ref[hh], NEG_INF)
        m = jnp.max(sc, axis=-1, keepdims=True)
        p = jnp.exp(sc - m)
        den = jnp.sum(p, axis=-1, keepdims=True)
        lse_ref[:, sl] = jnp.broadcast_to(m + jnp.log(den), (tq, HEAD_DIM))
        probs.append(p.astype(BF16))
        dens.append(den)
    for hh, sl in enumerate(heads):
        o = _dot(probs[hh], window(vp_ref, vc_ref, vn_ref, sl)) / dens[hh]
        o_ref[:, sl] = o.astype(o_ref.dtype)


def _attn_group(src, col0, rel_bias, gi, geom, *, tq=256):
    dil, rows, _ = src.shape
    assert dil == C_GROUPS[gi][1] and C_GROUPS[gi][0] // (2 * dil) == RADIUS
    tq = min(tq, min(geom[1], geom[2]) // dil)
    assert tq >= RADIUS and rows % tq == 0 and (geom[1] // dil) % tq == 0 and (geom[2] // dil) % tq == 0
    nq = rows // tq
    nk = tq + 2 * RADIUS
    rel = np.arange(nk)[None, :] - RADIUS - np.arange(tq)[:, None]
    bkt = np.where(np.abs(rel) <= RADIUS, _t5_bucket(rel * dil), -1).astype(np.int32)
    tab = rel_bias[:, gi * C_HPG:(gi + 1) * C_HPG].astype(F32).reshape(-1)
    cw = C_OUT

    def cur(col):
        return lambda r, qb: (r, qb, col)

    hpq = tq // RADIUS

    def prev(col):
        return lambda r, qb: (r, jnp.maximum(qb * hpq - 1, 0), col)

    def nxt(col):
        return lambda r, qb: (r, jnp.minimum((qb + 1) * hpq, rows // RADIUS - 1), col)

    blk = (None, tq, cw)
    halo = (None, RADIUS, cw)
    qcol, kcol, vcol = col0, col0 + 1, col0 + 2
    kern = functools.partial(_attn_kernel, tq=tq, dil=dil, geom=geom)
    return pl.pallas_call(
        kern,
        out_shape=(jax.ShapeDtypeStruct((dil, rows, cw), BF16), jax.ShapeDtypeStruct((dil, rows, cw), F32)),
        grid=(dil, nq),
        in_specs=[
            pl.BlockSpec(memory_space=pltpu.SMEM),
            pl.BlockSpec((tq, nk), lambda r, qb: (0, 0)),
            pl.BlockSpec(blk, cur(qcol)),
            pl.BlockSpec(halo, prev(kcol)), pl.BlockSpec(blk, cur(kcol)), pl.BlockSpec(halo, nxt(kcol)),
            pl.BlockSpec(halo, prev(vcol)), pl.BlockSpec(blk, cur(vcol)), pl.BlockSpec(halo, nxt(vcol)),
        ],
        out_specs=(pl.BlockSpec(blk, cur(0)), pl.BlockSpec(blk, cur(0))),
        scratch_shapes=[pltpu.VMEM((C_HPG, tq, nk), F32)],
        compiler_params=pltpu.CompilerParams(dimension_semantics=("arbitrary", "arbitrary")),
        name=f"attn_g{gi}",
    )(tab, jnp.asarray(bkt), src, src, src, src, src, src, src)


def _deint_kernel(x1_ref, x2_ref, o1_ref, o2_ref, scr_ref, *, tm):
    for x_ref, o_ref in ((x1_ref, o1_ref), (x2_ref, o2_ref)):
        dil = o_ref.shape[0]
        for cb in range(scr_ref.shape[0]):
            cols = slice(cb * LANE, (cb + 1) * LANE)
            scr_ref[cb] = x_ref[:, cols].astype(F32)
            for r in range(dil):
                o_ref[r, :, cols] = scr_ref[cb, pl.ds(r, tm // dil, stride=dil), :].astype(o_ref.dtype)


def _deinterleave(p, *, tm=512):
    t = p.shape[0]
    tm = min(tm, t)
    w = 3 * C_OUT
    d1, d2 = C_GROUPS[1][1], C_GROUPS[2][1]
    kern = functools.partial(_deint_kernel, tm=tm)
    return pl.pallas_call(
        kern,
        out_shape=(jax.ShapeDtypeStruct((d1, t // d1, w), BF16), jax.ShapeDtypeStruct((d2, t // d2, w), BF16)),
        grid=(t // tm,),
        in_specs=[pl.BlockSpec((tm, w), lambda i: (i, P_CQKV // w + 1)),
                  pl.BlockSpec((tm, w), lambda i: (i, P_CQKV // w + 2))],
        out_specs=(pl.BlockSpec((d1, tm // d1, w), lambda i: (0, i, 0)),
                   pl.BlockSpec((d2, tm // d2, w), lambda i: (0, i, 0))),
        scratch_shapes=[pltpu.VMEM((w // LANE, tm, LANE), F32)],
        compiler_params=pltpu.CompilerParams(dimension_semantics=("parallel",)),
        name="deinterleave",
    )(p, p)


def _mix_prep_kernel(of_ref, ob_ref, z_ref, hn_ref, b_ref, bp_ref, bn_ref, cw_ref,
                     o0_ref, o1_ref, o2_ref, l0_ref, l1_ref, l2_ref, y_ref, so1, so2, sl1, sl2, *, tm, geom, halo):
    o = of_ref[...].astype(F32) + ob_ref[...].astype(F32)
    z = z_ref[...].astype(F32)
    for hd in range(A_HEADS):
        sl = slice(hd * HEAD_DIM, (hd + 1) * HEAD_DIM)
        oh = o[:, sl]
        zh = z[:, sl]
        y = _rms(oh, hn_ref[...]) * (zh * _sigmoid(zh))
        y_ref[:, sl] = y.astype(BF16)

    has_prev, has_next = _halo_flags(tm, geom)

    def gated(ref, rows):
        return ref[rows, B_WIDTH:2 * B_WIDTH].astype(F32) * ref[rows, 2 * B_WIDTH:].astype(F32)

    u = gated(b_ref, slice(None))
    prev_row = gated(bp_ref, slice(halo - 1, halo)) * has_prev
    next_row = gated(bn_ref, slice(0, 1)) * has_next
    yb = b_ref[:, :B_WIDTH].astype(F32) * _conv3(u, prev_row, next_row, cw_ref)
    y_ref[:, A_WIDTH:A_WIDTH + B_WIDTH] = yb.astype(BF16)

    for src_ref, dst_ref in ((o1_ref, so1), (o2_ref, so2), (l1_ref, sl1), (l2_ref, sl2)):
        dil = src_ref.shape[0]
        for hd in range(C_HPG):
            for r in range(dil):
                dst_ref[hd, pl.ds(r, tm // dil, stride=dil), :] = (
                    src_ref[r, :, hd * HEAD_DIM:(hd + 1) * HEAD_DIM].astype(F32))
    for hd in range(C_HPG):
        sl = slice(hd * HEAD_DIM, (hd + 1) * HEAD_DIM)
        l0, l1, l2 = l0_ref[:, sl], sl1[hd], sl2[hd]
        m = jnp.maximum(jnp.maximum(l0, l1), l2)
        e0, e1, e2 = jnp.exp(l0 - m), jnp.exp(l1 - m), jnp.exp(l2 - m)
        num = e0 * o0_ref[:, sl].astype(F32) + e1 * so1[hd] + e2 * so2[hd]
        y_ref[:, A_WIDTH + B_WIDTH + hd * HEAD_DIM:A_WIDTH + B_WIDTH + (hd + 1) * HEAD_DIM] = (
            num / (e0 + e1 + e2)).astype(BF16)


def _mix_prep(p, o_f, o_b, attn, head_norm, conv_b, geom, *, tm=256, halo=16):
    t = p.shape[0]
    tm = min(tm, t)
    nb = tm // halo
    last = t // halo - 1
    w3 = 3 * B_WIDTH
    row = lambda w, col=0: pl.BlockSpec((tm, w), lambda i: (i, col))
    kern = functools.partial(_mix_prep_kernel, tm=tm, geom=geom, halo=halo)
    (o0, l0), (o1, l1), (o2, l2) = attn

    def res(a):
        dil = a.shape[0]
        if dil == 1:
            return pl.BlockSpec((None, tm, C_OUT), lambda i: (0, i, 0))
        return pl.BlockSpec((dil, tm // dil, C_OUT), lambda i: (0, i, 0))

    vmem = 2 * tm * (2 * A_WIDTH * 2 + A_WIDTH * 2 + w3 * 2 + 3 * C_OUT * 6 + Y_WIDTH * 2) + 12 * tm * A_WIDTH * 4
    return pl.pallas_call(
        kern,
        out_shape=jax.ShapeDtypeStruct((t, Y_WIDTH), BF16),
        grid=(t // tm,),
        in_specs=[
            row(A_WIDTH), row(A_WIDTH), row(A_WIDTH, P_Z // A_WIDTH),
            pl.BlockSpec((1, HEAD_DIM), lambda i: (0, 0)),
            row(w3, P_B3 // w3),
            pl.BlockSpec((halo, w3), lambda i: (jnp.maximum(i * nb - 1, 0), P_B3 // w3)),
            pl.BlockSpec((halo, w3), lambda i: (jnp.minimum((i + 1) * nb, last), P_B3 // w3)),
            pl.BlockSpec((3, B_WIDTH), lambda i: (0, 0)),
            res(o0), res(o1), res(o2), res(l0), res(l1), res(l2),
        ],
        out_specs=row(Y_WIDTH),
        scratch_shapes=[pltpu.VMEM((C_HPG, tm, HEAD_DIM), F32)] * 4,
        compiler_params=pltpu.CompilerParams(
            dimension_semantics=("parallel",), vmem_limit_bytes=_vmem_limit(vmem)),
        name="mix_prep",
    )(o_f, o_b, p, head_norm.reshape(1, HEAD_DIM), p, p, p, conv_b, o0, o1, o2, l0, l1, l2)


def _merge_kernel(*refs, bounds):
    nsrc = len(bounds) - 1
    h_refs = refs[:nsrc]
    y_ref, g_ref, wa_ref, wb_ref, wc_ref, wo_ref, o_ref = refs[nsrc:]
    d = D_MODEL
    merged = _sigmoid(g_ref[:, 0:d].astype(F32)) * _dot(y_ref[:, :A_WIDTH], wa_ref[...])
    merged += _sigmoid(g_ref[:, d:2 * d].astype(F32)) * _dot(y_ref[:, A_WIDTH:A_WIDTH + B_WIDTH], wb_ref[...])
    merged += _sigmoid(g_ref[:, 2 * d:].astype(F32)) * _dot(y_ref[:, A_WIDTH + B_WIDTH:], wc_ref[...])
    delta = _dot(merged.astype(BF16), wo_ref[...])

    def residual(h_ref):
        o_ref[...] = h_ref[...] + delta

    _on_owner(h_refs, bounds, residual)


def _merge(y, p, hs, wa, wb, wc, wo, layer, *, tm=256):
    t = y.shape[0]
    tm = min([tm] + [a.shape[0] for a in hs])
    d = D_MODEL
    h_specs, bounds = _row_specs(hs, tm, d)
    const = lambda shape: pl.BlockSpec((None,) + shape, lambda i: (layer, 0, 0), pipeline_mode=pl.Buffered(1))
    wbytes = (A_WIDTH + B_WIDTH + C_OUT + d) * d * 2
    vmem = wbytes + 2 * tm * (Y_WIDTH * 2 + 3 * d * 2 + len(hs) * d * 4 + d * 4) + 6 * tm * d * 4
    return pl.pallas_call(
        functools.partial(_merge_kernel, bounds=bounds),
        out_shape=jax.ShapeDtypeStruct((t, d), F32),
        grid=(t // tm,),
        in_specs=h_specs + [
            pl.BlockSpec((tm, Y_WIDTH), lambda i: (i, 0)),
            pl.BlockSpec((tm, 3 * d), lambda i: (i, P_GATES // (3 * d))),
            const((A_WIDTH, d)), const((B_WIDTH, d)), const((C_OUT, d)), const((d, d)),
        ],
        out_specs=pl.BlockSpec((tm, d), lambda i: (i, 0)),
        compiler_params=pltpu.CompilerParams(
            dimension_semantics=("parallel",), vmem_limit_bytes=_vmem_limit(vmem)),
        name="merge",
    )(*hs, y, p, wa, wb, wc, wo)


def _mlp_kernel(h_ref, nw_ref, wu_ref, wd_ref, fw_ref, *refs, bounds, final):
    nout = len(bounds) - 1
    o_refs = refs[:nout]
    xn_ref, acc_ref = refs[nout:]
    f = pl.program_id(1)

    @pl.when(f == 0)
    def _():
        xn_ref[...] = _rms(h_ref[...], nw_ref[...]).astype(BF16)
        acc_ref[...] = jnp.zeros_like(acc_ref)

    up = jnp.maximum(_dot(xn_ref[...], wu_ref[...]), 0.0)
    acc_ref[...] += _dot((up * up).astype(BF16), wd_ref[...])

    def emit(o_ref):
        hn = h_ref[...] + acc_ref[...]
        o_ref[...] = _rms(hn, fw_ref[...]) if final else hn

    _on_owner(o_refs, bounds, emit, extra_cond=(f == pl.num_programs(1) - 1))


def _mlp(h, norm_w, wu, wd, final_w, layer, *, final, out_rows, tm=512, tf=1024):
    t = h.shape[0]
    tm = min([tm] + list(out_rows))
    d = D_MODEL
    outs = tuple(jax.ShapeDtypeStruct((r, d), F32) for r in out_rows)
    o_specs, bounds = _row_specs(outs, tm, d)
    kern = functools.partial(_mlp_kernel, bounds=bounds, final=final)
    vmem = (4 * tm * d * 4 + tm * d * 2 + tm * d * 4 + 4 * d * tf * 2 + 3 * tm * tf * 4
            + 2 * (len(outs) - 1) * tm * d * 4)
    return pl.pallas_call(
        kern,
        out_shape=outs,
        grid=(t // tm, D_FF // tf),
        in_specs=[
            pl.BlockSpec((tm, d), lambda i, f: (i, 0)),
            pl.BlockSpec((1, d), lambda i, f: (0, 0)),
            pl.BlockSpec((None, d, tf), lambda i, f: (layer, 0, f)),
            pl.BlockSpec((None, tf, d), lambda i, f: (layer, f, 0)),
            pl.BlockSpec((1, d), lambda i, f: (0, 0)),
        ],
        out_specs=tuple(o_specs),
        scratch_shapes=[pltpu.VMEM((tm, d), BF16), pltpu.VMEM((tm, d), F32)],
        compiler_params=pltpu.CompilerParams(
            dimension_semantics=("arbitrary", "arbitrary"), vmem_limit_bytes=_vmem_limit(vmem + (8 << 20))),
        name="mlp",
    )(h, norm_w.reshape(1, d), wu, wd, final_w.reshape(1, d))


PACK_ROWS = 512


def _w_in_block_sources():
    src = [P_QKV + PACK_ROWS * b for b in range(3072 // PACK_ROWS)]
    src += [4128 + PACK_ROWS * b for b in range(3072 // PACK_ROWS)]
    src += [3072 + PACK_ROWS * b for b in range(1024 // PACK_ROWS)]
    src += [4096]
    c0 = 7200
    src += [c0 + part * C_WIDTH + g * C_OUT for g in range(len(C_GROUPS)) for part in range(3)]
    src += [11808 + PACK_ROWS * b for b in range(3 * D_MODEL // PACK_ROWS)]
    assert len(src) == P_WIDTH // PACK_ROWS and P_AB == 14 * PACK_ROWS and P_CQKV == 15 * PACK_ROWS
    return np.asarray(src, np.int32)


def _cast_kernel(w_ref, o_ref):
    o_ref[...] = w_ref[...].astype(BF16)


def _pack_kernel(src_ref, w_ref, o_ref):
    del src_ref
    o_ref[...] = w_ref[0].astype(BF16)


def _pack_w_in(w_t):
    nl, cols, d = w_t.shape
    src = _w_in_block_sources()
    assert int(src.max()) + PACK_ROWS <= cols
    return pl.pallas_call(
        _pack_kernel,
        out_shape=jax.ShapeDtypeStruct((nl, P_WIDTH, d), BF16),
        grid_spec=pltpu.PrefetchScalarGridSpec(
            num_scalar_prefetch=1,
            grid=(nl, P_WIDTH // PACK_ROWS),
            in_specs=[pl.BlockSpec((pl.Element(1), pl.Element(PACK_ROWS), pl.Element(d)),
                                   lambda l, b, src_ref: (l, pl.multiple_of(src_ref[b], 8), 0))],
            out_specs=pl.BlockSpec((None, PACK_ROWS, d), lambda l, b, src_ref: (l, b, 0)),
        ),
        compiler_params=pltpu.CompilerParams(dimension_semantics=("parallel", "parallel")),
        name="pack_w_in",
    )(jnp.asarray(src), w_t)


def _cast_bf16(w, *, block_bytes=4 << 20):
    nl, r, c = w.shape
    tr = max(8, min(r, block_bytes // (c * 4)))
    assert r % tr == 0
    return pl.pallas_call(
        _cast_kernel,
        out_shape=jax.ShapeDtypeStruct(w.shape, BF16),
        grid=(nl, r // tr),
        in_specs=[pl.BlockSpec((None, tr, c), lambda l, i: (l, i, 0))],
        out_specs=pl.BlockSpec((None, tr, c), lambda l, i: (l, i, 0)),
        compiler_params=pltpu.CompilerParams(dimension_semantics=("parallel", "parallel")),
        name="cast_bf16",
    )(w)


def _trunk(xs, geom, rel_bias, norm_mix, w_in, conv_a, a_log, dt_bias, head_norm, conv_b,
           w_br_a, w_br_b, w_br_c, w_out, norm_mlp, w_up, w_down, norm_final):
    depth = w_in.shape[0]
    rows = tuple(a.shape[0] for a in xs)
    t = sum(rows)
    hs = list(xs)
    w_in_b = _pack_w_in(jnp.swapaxes(w_in, 1, 2))
    wa_b, wb_b, wc_b, wo_b, wu_b, wd_b = (_cast_bf16(w) for w in (w_br_a, w_br_b, w_br_c, w_out, w_up, w_down))
    for layer in range(depth):
        last = layer == depth - 1
        p, ab = _in_proj(hs, norm_mix[layer], w_in_b, layer)
        qkv, gb = _gdn_pre(p, ab, conv_a[layer], a_log[layer], dt_bias[layer], geom)
        gbt3 = gb[:, :32].reshape(t // GDN_CHUNK, GDN_CHUNK, 32).transpose(0, 2, 1)
        o_f, o_b = _gdn_scan(qkv, gb, gbt3, geom)
        r1, r2 = _deinterleave(p)
        attn = [_attn_group(p.reshape(1, t, P_WIDTH), P_CQKV // C_OUT, rel_bias, 0, geom),
                _attn_group(r1, 0, rel_bias, 1, geom),
                _attn_group(r2, 0, rel_bias, 2, geom)]
        y = _mix_prep(p, o_f, o_b, attn, head_norm[layer], conv_b[layer], geom)
        h = _merge(y, p, hs, wa_b, wb_b, wc_b, wo_b, layer)
        hs = list(_mlp(h, norm_mlp[layer], wu_b, wd_b, norm_final, layer,
                       final=last, out_rows=rows if last else (t,)))
    return hs


def kernel(x_prompt, x_sample, rel_bias, norm_mix, w_in, conv_a, a_log, dt_bias, head_norm, conv_b,
           w_br_a, w_br_b, w_br_c, w_out, norm_mlp, w_up, w_down, norm_final):
    bp, sp, d = x_prompt.shape
    bs, ss, _ = x_sample.shape
    tp = bp * sp
    geom = (tp, sp, ss)
    xs = (x_prompt.reshape(tp, d), x_sample.reshape(bs * ss, d))
    y_p, y_s = _trunk(xs, geom, rel_bias, norm_mix, w_in, conv_a, a_log, dt_bias, head_norm, conv_b,
                      w_br_a, w_br_b, w_br_c, w_out, norm_mlp, w_up, w_down, norm_final)
    return y_p.reshape(bp, sp, d), y_s.reshape(bs, ss, d)
```

```python
import functools
import math

import jax
import jax.numpy as jnp
import numpy as np
from jax import lax
from jax.experimental import pallas as pl
from jax.experimental.pallas import tpu as pltpu

F32 = jnp.float32
BF16 = jnp.bfloat16

LANE = 128
VMEM_BYTES_V7X = 64 * 1024 * 1024

D_MODEL = 2048
A_HEADS = 8
HEAD_DIM = 128
A_WIDTH = A_HEADS * HEAD_DIM
GDN_CHUNK = 64
B_WIDTH = 1024
C_GROUPS = ((128, 1), (512, 4), (2048, 16))
C_HPG = 4
C_HEADS = 12
C_WIDTH = C_HEADS * HEAD_DIM
C_OUT = C_HPG * HEAD_DIM
N_BUCKETS = 32
REL_MAX_DIST = 2048
RADIUS = 64
D_FF = 4 * D_MODEL
EPS = 1e-6
NEG_INF = -1e30

P_QKV = 0
P_B3 = 3072
P_Z = 6144
P_AB = 7168
P_CQKV = 7680
P_GATES = 12288
P_WIDTH = 18432


def _vmem_limit(nbytes):
    return int(min(max(nbytes, 16 * 1024 * 1024), VMEM_BYTES_V7X - 8 * 1024 * 1024))


def _seq_bounds(t0, geom):
    tp, sp, ss = geom
    in_p = t0 < tp
    length = jnp.where(in_p, sp, ss)
    base = jnp.where(in_p, 0, tp)
    lo = base + ((t0 - base) // length) * length
    return lo, lo + length


def _rms(x, w):
    ms = jnp.mean(x * x, axis=-1, keepdims=True)
    return x * lax.rsqrt(ms + EPS) * w


def _sigmoid(x):
    return 1.0 / (1.0 + jnp.exp(-x))


def _dot(a, b):
    return jnp.dot(a, b, preferred_element_type=F32)


def _dot_nt(a, b):
    return lax.dot_general(a, b, (((1,), (1,)), ((), ())), preferred_element_type=F32)


def _dot_tn(a, b):
    return lax.dot_general(a, b, (((0,), (0,)), ((), ())), preferred_element_type=F32)


def _row_specs(srcs, tm, width, single_buffer=False):
    specs, bounds = [], [0]
    for a in srcs:
        n, off = a.shape[0] // tm, bounds[-1]
        assert a.shape[0] % tm == 0
        mode = dict(pipeline_mode=pl.Buffered(1)) if single_buffer and len(srcs) > 1 else {}
        specs.append(pl.BlockSpec((tm, width), lambda i, *_, off=off, n=n: (jnp.clip(i - off, 0, n - 1), 0), **mode))
        bounds.append(off + n)
    return specs, tuple(bounds)


def _on_owner(refs, bounds, fn, extra_cond=None):
    i = pl.program_id(0)
    for k, ref in enumerate(refs):
        cond = extra_cond
        if len(refs) > 1:
            own = (i >= bounds[k]) & (i < bounds[k + 1])
            cond = own if cond is None else cond & own
        if cond is None:
            fn(ref)
        else:
            pl.when(cond)(functools.partial(fn, ref))


def _inproj_kernel(*refs, bounds, ab_tile, ab_off):
    nsrc = len(bounds) - 1
    x_refs = refs[:nsrc]
    nw_ref, w_ref, p_ref, ab_ref, xn_ref = refs[nsrc:]
    j = pl.program_id(1)

    def norm(x_ref):
        xn_ref[...] = _rms(x_ref[...], nw_ref[...]).astype(BF16)

    _on_owner(x_refs, bounds, norm, extra_cond=(j == 0))
    acc = _dot_nt(xn_ref[...], w_ref[...])
    p_ref[...] = acc.astype(BF16)

    @pl.when(j == ab_tile)
    def _():
        ab_ref[...] = acc[:, ab_off:ab_off + LANE]


def _in_proj(hs, norm_w, w_packed, layer, *, tm=1024, tn=1024):
    t = sum(a.shape[0] for a in hs)
    tm = min([tm] + [a.shape[0] for a in hs])
    assert P_WIDTH % tn == 0
    x_specs, bounds = _row_specs(hs, tm, D_MODEL, single_buffer=True)
    kern = functools.partial(_inproj_kernel, bounds=bounds, ab_tile=P_AB // tn, ab_off=P_AB % tn)
    vmem = 2 * tm * D_MODEL * 4 + tm * D_MODEL * 2 + 2 * D_MODEL * tn * 2 + 2 * tm * tn * 2 + 2 * tm * tn * 4
    return pl.pallas_call(
        kern,
        out_shape=(jax.ShapeDtypeStruct((t, P_WIDTH), BF16), jax.ShapeDtypeStruct((t, LANE), F32)),
        grid=(t // tm, P_WIDTH // tn),
        in_specs=x_specs + [
            pl.BlockSpec((1, D_MODEL), lambda i, j: (0, 0)),
            pl.BlockSpec((None, tn, D_MODEL), lambda i, j: (layer, j, 0)),
        ],
        out_specs=(
            pl.BlockSpec((tm, tn), lambda i, j: (i, j)),
            pl.BlockSpec((tm, LANE), lambda i, j: (i, 0)),
        ),
        scratch_shapes=[pltpu.VMEM((tm, D_MODEL), BF16)],
        compiler_params=pltpu.CompilerParams(
            dimension_semantics=("parallel", "arbitrary"), vmem_limit_bytes=_vmem_limit(vmem + (8 << 20))),
        name="in_proj",
    )(*hs, norm_w.reshape(1, D_MODEL), w_packed)


def _conv3(x, prev_row, next_row, w_ref):
    tm = x.shape[0]
    rows = lax.broadcasted_iota(jnp.int32, (tm, 1), 0)
    x_m1 = jnp.where(rows == 0, prev_row, pltpu.roll(x, 1, 0))
    x_p1 = jnp.where(rows == tm - 1, next_row, pltpu.roll(x, tm - 1, 0))
    return x_m1 * w_ref[0:1, :] + x * w_ref[1:2, :] + x_p1 * w_ref[2:3, :]


def _halo_flags(tm, geom):
    t0 = pl.program_id(0) * tm
    lo, hi = _seq_bounds(t0, geom)
    return (t0 > lo).astype(F32), (t0 + tm < hi).astype(F32)


def _gdn_pre_kernel(x_ref, xp_ref, xn_ref, cw_ref, ab_ref, nalog_ref, dtb_ref, o_ref, gb_ref, *, tm, geom, halo):
    has_prev, has_next = _halo_flags(tm, geom)
    x = x_ref[...].astype(F32)
    prev_row = xp_ref[halo - 1:halo, :].astype(F32) * has_prev
    next_row = xn_ref[0:1, :].astype(F32) * has_next
    y = _conv3(x, prev_row, next_row, cw_ref)
    y = y * _sigmoid(y)
    for hd in range(2 * A_HEADS):
        sl = slice(hd * HEAD_DIM, (hd + 1) * HEAD_DIM)
        yh = y[:, sl]
        inv = lax.rsqrt(jnp.sum(yh * yh, axis=-1, keepdims=True) + EPS)
        if hd < A_HEADS:
            inv = inv * (HEAD_DIM ** -0.5)
        o_ref[:, sl] = (yh * inv).astype(BF16)
    o_ref[:, 2 * A_WIDTH:] = y[:, 2 * A_WIDTH:].astype(BF16)

    ab = ab_ref[...]
    xs = ab + dtb_ref[...]
    softplus = jnp.maximum(xs, 0.0) + jnp.log(1.0 + jnp.exp(-jnp.abs(xs)))
    g = nalog_ref[...] * softplus
    lane = lax.broadcasted_iota(jnp.int32, ab.shape, 1)
    gb_ref[...] = jnp.where(lane < 2 * A_HEADS, g, _sigmoid(ab))


def _gdn_pre(p, ab, conv_w, a_log, dt_bias, geom, *, tm=256, halo=16):
    t = p.shape[0]
    tm = min(tm, t)
    w = 3 * A_WIDTH
    nb = tm // halo
    last = t // halo - 1
    nalog = jnp.zeros((1, LANE), F32).at[0, :2 * A_HEADS].set(-jnp.exp(a_log.reshape(-1)))
    dtb = jnp.zeros((1, LANE), F32).at[0, :2 * A_HEADS].set(dt_bias.reshape(-1))
    kern = functools.partial(_gdn_pre_kernel, tm=tm, geom=geom, halo=halo)
    vmem = 2 * (tm * w * 2 * 2 + 2 * halo * w * 2) + 10 * tm * w * 4
    return pl.pallas_call(
        kern,
        out_shape=(jax.ShapeDtypeStruct((t, w), BF16), jax.ShapeDtypeStruct((t, LANE), F32)),
        grid=(t // tm,),
        in_specs=[
            pl.BlockSpec((tm, w), lambda i: (i, P_QKV // w)),
            pl.BlockSpec((halo, w), lambda i: (jnp.maximum(i * nb - 1, 0), P_QKV // w)),
            pl.BlockSpec((halo, w), lambda i: (jnp.minimum((i + 1) * nb, last), P_QKV // w)),
            pl.BlockSpec((3, w), lambda i: (0, 0)),
            pl.BlockSpec((tm, LANE), lambda i: (i, 0)),
            pl.BlockSpec((1, LANE), lambda i: (0, 0)),
            pl.BlockSpec((1, LANE), lambda i: (0, 0)),
        ],
        out_specs=(
            pl.BlockSpec((tm, w), lambda i: (i, 0)),
            pl.BlockSpec((tm, LANE), lambda i: (i, 0)),
        ),
        compiler_params=pltpu.CompilerParams(
            dimension_semantics=("parallel",), vmem_limit_bytes=_vmem_limit(vmem)),
        name="gdn_pre",
    )(p, p, p, conv_w, ab, nalog, dtb)


def _split3(x):
    x1 = x.astype(BF16)
    r1 = x - x1.astype(F32)
    x2 = r1.astype(BF16)
    r2 = r1 - x2.astype(F32)
    return x1, x2, r2.astype(BF16)


def _gdn_scan_kernel(qf_ref, kf_ref, vf_ref, gf_ref, grf_ref, qb_ref, kb_ref, vb_ref, gbk_ref, grb_ref,
                     of_ref, ob_ref, s_ref, wq_s, u_s, lhs2_s, egl_s, *, rb, nblk, geom):
    c = GDN_CHUNK
    nc = rb // c
    i = pl.program_id(0)
    t0f = i * rb
    t0b = (nblk - 1 - i) * rb
    lo_f, _ = _seq_bounds(t0f, geom)
    _, hi_b = _seq_bounds(t0b, geom)

    @pl.when(t0f == lo_f)
    def _():
        s_ref[0:A_HEADS] = jnp.zeros((A_HEADS, HEAD_DIM, HEAD_DIM), F32)

    @pl.when(t0b + rb == hi_b)
    def _():
        s_ref[A_HEADS:2 * A_HEADS] = jnp.zeros((A_HEADS, HEAD_DIM, HEAD_DIM), F32)

    ri = lax.broadcasted_iota(jnp.int32, (c, c), 0)
    ci = lax.broadcasted_iota(jnp.int32, (c, c), 1)
    lower_incl = ri >= ci
    upper_incl = ri <= ci
    tri_lo = jnp.where(lower_incl, 1.0, 0.0).astype(BF16)
    tri_up = jnp.where(upper_incl, 1.0, 0.0).astype(BF16)
    bd16 = (ri // 16) == (ci // 16)
    off32 = ((ri // 32) == (ci // 32)) & jnp.logical_not(bd16)
    off64 = (ri // 32) != (ci // 32)
    eye = jnp.where(ri == ci, 1.0, 0.0).astype(F32)
    bf = lambda x: x.astype(BF16)

    sides = (
        (0, qf_ref, kf_ref, vf_ref, gf_ref, grf_ref, tri_lo, tri_up, lower_incl, ri > ci),
        (1, qb_ref, kb_ref, vb_ref, gbk_ref, grb_ref, tri_up, tri_lo, upper_incl, ri < ci),
    )

    def prep(jp, carry):
        chains = []
        for direction, q_ref, k_ref, v_ref, g_ref, gr_ref, m_col, m_row, incl, strict in sides:
            for cidx in (2 * jp, 2 * jp + 1):
                r0 = pl.multiple_of(cidx * c, c)
                gcol = g_ref[pl.ds(r0, c), :]
                grow = gr_ref[cidx]
                g1, g2, g3 = _split3(gcol)
                gc_col = _dot(m_col, g1) + _dot(m_col, g2) + _dot(m_col, g3)
                h1, h2, h3 = _split3(grow)
                gc_row = _dot(h1, m_row) + _dot(h2, m_row) + _dot(h3, m_row)
                for hh in range(A_HEADS):
                    ch = direction * A_HEADS + hh
                    sl = slice(hh * HEAD_DIM, (hh + 1) * HEAD_DIM)
                    gcr = gc_row[ch:ch + 1, :]
                    chains.append(dict(
                        ch=ch, cidx=cidx, incl=incl, strict=strict,
                        q=q_ref[pl.ds(r0, c), sl], k=k_ref[pl.ds(r0, c), sl], v=v_ref[pl.ds(r0, c), sl],
                        gcr=gcr, beta_row=grow[2 * A_HEADS + ch:2 * A_HEADS + ch + 1, :],
                        gcc_b=jnp.broadcast_to(gc_col[:, ch:ch + 1], (c, HEAD_DIM)),
                        g_last=gcr[:, c - 1:c] if direction == 0 else gcr[:, 0:1]))

        gram = [_dot_nt(jnp.concatenate([x["q"], x["k"]], axis=0), x["k"]) for x in chains]
        zs, e32, e64 = [], [], []
        for x, gm in zip(chains, gram):
            decay = jnp.exp(jnp.where(x["incl"], x["gcc_b"][:, :c] - x["gcr"], NEG_INF))
            dec_beta = decay * x["beta_row"]
            k_t = jnp.transpose(x["k"].astype(F32))
            k_dec_t = k_t * (jnp.exp(x["g_last"] - x["gcr"]) * x["beta_row"])
            lhs2_s[x["cidx"], x["ch"]] = jnp.concatenate([bf(gm[:c] * dec_beta), bf(k_dec_t)], axis=0)
            egl_s[x["cidx"], x["ch"]] = jnp.broadcast_to(jnp.exp(x["g_last"]), (1, HEAD_DIM))
            l_mat = jnp.where(x["strict"], gm[c:] * dec_beta, 0.0)
            zs.append(jnp.where(bd16, -l_mat, 0.0))
            e32.append(bf(jnp.where(off32, l_mat, 0.0)))
            e64.append(bf(jnp.where(off64, l_mat, 0.0)))

        zb = [bf(z) for z in zs]
        xs = [eye + z for z in zs]
        pw = [bf(_dot(b, b)) for b in zb]
        for step in range(3):
            nxt = [bf(_dot(p, p)) for p in pw] if step < 2 else None
            xs = [x + _dot(bf(x), p) for x, p in zip(xs, pw)]
            pw = nxt
        for e in (e32, e64):
            xb = [bf(x) for x in xs]
            ts = [bf(_dot(b, a)) for b, a in zip(xb, e)]
            xs = [x - _dot(t, b) for x, t, b in zip(xs, ts, xb)]

        for x, t_inv in zip(chains, xs):
            eg = jnp.exp(x["gcc_b"])
            rhs = jnp.concatenate([x["v"], bf(x["k"].astype(F32) * eg)], axis=1)
            uw = _dot(bf(t_inv), rhs)
            u_s[x["cidx"], x["ch"]] = uw[:, :HEAD_DIM]
            wq_s[x["cidx"], x["ch"]] = jnp.concatenate(
                [bf(uw[:, HEAD_DIM:]), bf(x["q"].astype(F32) * eg)], axis=0)
        return carry

    lax.fori_loop(0, nc // 2, prep, 0)

    def scan(j, carry):
        chains = [(d * A_HEADS + hh, cidx, o_ref, slice(hh * HEAD_DIM, (hh + 1) * HEAD_DIM))
                  for d, cidx, o_ref in ((0, j, of_ref), (1, nc - 1 - j, ob_ref)) for hh in range(A_HEADS)]
        states = [s_ref[ch] for ch, _, _, _ in chains]
        ws_qs = [_dot(wq_s[cidx, ch], bf(s)) for (ch, cidx, _, _), s in zip(chains, states)]
        v_new = [bf(u_s[cidx, ch] - r[:c]) for (ch, cidx, _, _), r in zip(chains, ws_qs)]
        upd = [_dot(lhs2_s[cidx, ch], vn) for (ch, cidx, _, _), vn in zip(chains, v_new)]
        for (ch, cidx, o_ref, sl), s, r, r2 in zip(chains, states, ws_qs, upd):
            r0 = pl.multiple_of(cidx * c, c)
            o_ref[pl.ds(r0, c), sl] = (r[c:] + r2[:c]).astype(o_ref.dtype)
            s_ref[ch] = s * egl_s[cidx, ch] + r2[c:]
        return carry

    lax.fori_loop(0, nc, scan, 0)


def _gdn_scan(qkv, gb, gbt3, geom, *, rb=512):
    t = qkv.shape[0]
    rb = min(rb, t)
    nblk = t // rb
    ncb = rb // GDN_CHUNK
    kern = functools.partial(_gdn_scan_kernel, rb=rb, nblk=nblk, geom=geom)

    def fwd(col):
        return lambda i: (i, col)

    def bwd(col):
        return lambda i: (nblk - 1 - i, col)

    in_specs = []
    for mk in (fwd, bwd):
        in_specs += [
            pl.BlockSpec((rb, A_WIDTH), mk(0)),
            pl.BlockSpec((rb, A_WIDTH), mk(1)),
            pl.BlockSpec((rb, A_WIDTH), mk(2)),
            pl.BlockSpec((rb, LANE), mk(0)),
            pl.BlockSpec((ncb, 32, GDN_CHUNK), (lambda i: (i, 0, 0)) if mk is fwd else (lambda i: (nblk - 1 - i, 0, 0))),
        ]
    nch = 2 * A_HEADS
    assert ncb % 2 == 0
    vmem = (2 * (6 * rb * A_WIDTH * 2 + 2 * rb * LANE * 4 + 2 * rb * A_WIDTH * 2) + nch * HEAD_DIM * HEAD_DIM * 4
            + ncb * nch * HEAD_DIM * (2 * GDN_CHUNK * 2 + GDN_CHUNK * 4 + (GDN_CHUNK + HEAD_DIM) * 2 + 8 * 4))
    return pl.pallas_call(
        kern,
        out_shape=(jax.ShapeDtypeStruct((t, A_WIDTH), BF16), jax.ShapeDtypeStruct((t, A_WIDTH), BF16)),
        grid=(nblk,),
        in_specs=in_specs,
        out_specs=(pl.BlockSpec((rb, A_WIDTH), fwd(0)), pl.BlockSpec((rb, A_WIDTH), bwd(0))),
        scratch_shapes=[
            pltpu.VMEM((nch, HEAD_DIM, HEAD_DIM), F32),
            pltpu.VMEM((ncb, nch, 2 * GDN_CHUNK, HEAD_DIM), BF16),
            pltpu.VMEM((ncb, nch, GDN_CHUNK, HEAD_DIM), F32),
            pltpu.VMEM((ncb, nch, GDN_CHUNK + HEAD_DIM, GDN_CHUNK), BF16),
            pltpu.VMEM((ncb, nch, 1, HEAD_DIM), F32),
        ],
        compiler_params=pltpu.CompilerParams(
            dimension_semantics=("arbitrary",), vmem_limit_bytes=_vmem_limit(vmem + (16 << 20))),
        name="gdn_scan",
    )(qkv, qkv, qkv, gb, gbt3, qkv, qkv, qkv, gb, gbt3)


def _t5_bucket(rel):
    half = N_BUCKETS // 2
    exact = half // 2
    ret = np.where(rel > 0, half, 0)
    n = np.abs(rel)
    large = exact + (np.log(np.maximum(n, 1) / exact) / math.log(REL_MAX_DIST / exact) * (half - exact)).astype(np.int32)
    large = np.minimum(large, half - 1)
    return (ret + np.where(n < exact, n, large)).astype(np.int32)


def _attn_kernel(tab_ref, bkt_ref, p_ref, c_ref, n_ref, o_ref, lse_ref, bias_ref, *, tq, dil, geom):
    r = pl.program_id(0)
    qb = pl.program_id(1)
    sq = RADIUS
    nk = 3 * sq

    @pl.when((r == 0) & (qb == 0))
    def _():
        bkt = bkt_ref[...]
        for hh in range(C_HPG):
            acc = jnp.full((sq, nk), NEG_INF, F32)
            for b in range(N_BUCKETS):
                acc = jnp.where(bkt == b, tab_ref[b * C_HPG + hh], acc)
            bias_ref[hh] = acc

    lo, hi = _seq_bounds(qb * tq * dil, geom)
    scale = HEAD_DIM ** -0.5
    nsb = tq // sq

    def rows(part, hh, s):
        cols = slice(part * C_OUT + hh * HEAD_DIM, part * C_OUT + (hh + 1) * HEAD_DIM)
        if s < 0:
            return p_ref[:, cols]
        if s >= nsb:
            return n_ref[:, cols]
        return c_ref[s * sq:(s + 1) * sq, cols]

    def window(part, hh, s):
        return jnp.concatenate([rows(part, hh, s - 1), rows(part, hh, s), rows(part, hh, s + 1)], axis=0)

    probs = [(s, hh) for s in range(nsb) for hh in range(C_HPG)]
    scores = [_dot_nt(rows(0, hh, s), window(1, hh, s)) for s, hh in probs]
    ps, dens = [], []
    for (s, hh), sc in zip(probs, scores):
        krow = qb * tq + (s - 1) * sq + lax.broadcasted_iota(jnp.int32, (1, nk), 1)
        in_seq = (krow * dil >= lo) & (krow * dil < hi)
        sc = jnp.where(in_seq, sc * scale + bias_ref[hh], NEG_INF)
        m = jnp.max(sc, axis=-1, keepdims=True)
        p = jnp.exp(sc - m)
        den = jnp.sum(p, axis=-1, keepdims=True)
        lse_ref[s * sq:(s + 1) * sq, hh * HEAD_DIM:(hh + 1) * HEAD_DIM] = jnp.broadcast_to(
            m + jnp.log(den), (sq, HEAD_DIM))
        ps.append(p.astype(BF16))
        dens.append(den)
    for (s, hh), p, den in zip(probs, ps, dens):
        o = _dot(p, window(2, hh, s)) / den
        o_ref[s * sq:(s + 1) * sq, hh * HEAD_DIM:(hh + 1) * HEAD_DIM] = o.astype(o_ref.dtype)


def _attn_group(src, col0, rel_bias, gi, geom, *, tq=512):
    dil, rows, _ = src.shape
    assert dil == C_GROUPS[gi][1] and C_GROUPS[gi][0] // (2 * dil) == RADIUS
    tq = min(tq, min(geom[1], geom[2]) // dil)
    assert tq >= RADIUS and rows % tq == 0 and (geom[1] // dil) % tq == 0 and (geom[2] // dil) % tq == 0
    nq = rows // tq
    sq, nk = RADIUS, 3 * RADIUS
    rel = np.arange(nk)[None, :] - RADIUS - np.arange(sq)[:, None]
    bkt = np.where(np.abs(rel) <= RADIUS, _t5_bucket(rel * dil), -1).astype(np.int32)
    tab = rel_bias[:, gi * C_HPG:(gi + 1) * C_HPG].astype(F32).reshape(-1)
    cw = C_OUT
    assert col0 % 3 == 0
    qkv_col = col0 // 3
    hpq = tq // RADIUS
    kern = functools.partial(_attn_kernel, tq=tq, dil=dil, geom=geom)
    return pl.pallas_call(
        kern,
        out_shape=(jax.ShapeDtypeStruct((dil, rows, cw), BF16), jax.ShapeDtypeStruct((dil, rows, cw), F32)),
        grid=(dil, nq),
        in_specs=[
            pl.BlockSpec(memory_space=pltpu.SMEM),
            pl.BlockSpec((sq, nk), lambda r, qb: (0, 0)),
            pl.BlockSpec((None, RADIUS, 3 * cw), lambda r, qb: (r, jnp.maximum(qb * hpq - 1, 0), qkv_col)),
            pl.BlockSpec((None, tq, 3 * cw), lambda r, qb: (r, qb, qkv_col)),
            pl.BlockSpec((None, RADIUS, 3 * cw),
                         lambda r, qb: (r, jnp.minimum((qb + 1) * hpq, rows // RADIUS - 1), qkv_col)),
        ],
        out_specs=(pl.BlockSpec((None, tq, cw), lambda r, qb: (r, qb, 0)),
                   pl.BlockSpec((None, tq, cw), lambda r, qb: (r, qb, 0))),
        scratch_shapes=[pltpu.VMEM((C_HPG, sq, nk), F32)],
        compiler_params=pltpu.CompilerParams(
            dimension_semantics=("arbitrary", "arbitrary"),
            vmem_limit_bytes=_vmem_limit(2 * (tq + 2 * RADIUS) * 3 * cw * 2 + 2 * tq * cw * 6
                                         + C_HPG * tq * nk * 4 * 4 + (8 << 20))),
        name=f"attn_g{gi}",
    )(tab, jnp.asarray(bkt), src, src, src)


def _deint_kernel(x1_ref, x2_ref, o1_ref, o2_ref, scr_ref, *, tm):
    for x_ref, o_ref in ((x1_ref, o1_ref), (x2_ref, o2_ref)):
        dil = o_ref.shape[0]
        for cb in range(scr_ref.shape[0]):
            cols = slice(cb * LANE, (cb + 1) * LANE)
            scr_ref[cb] = x_ref[:, cols].astype(F32)
            for r in range(dil):
                o_ref[r, :, cols] = scr_ref[cb, pl.ds(r, tm // dil, stride=dil), :].astype(o_ref.dtype)


def _deinterleave(p, *, tm=512):
    t = p.shape[0]
    tm = min(tm, t)
    w = 3 * C_OUT
    d1, d2 = C_GROUPS[1][1], C_GROUPS[2][1]
    kern = functools.partial(_deint_kernel, tm=tm)
    return pl.pallas_call(
        kern,
        out_shape=(jax.ShapeDtypeStruct((d1, t // d1, w), BF16), jax.ShapeDtypeStruct((d2, t // d2, w), BF16)),
        grid=(t // tm,),
        in_specs=[pl.BlockSpec((tm, w), lambda i: (i, P_CQKV // w + 1)),
                  pl.BlockSpec((tm, w), lambda i: (i, P_CQKV // w + 2))],
        out_specs=(pl.BlockSpec((d1, tm // d1, w), lambda i: (0, i, 0)),
                   pl.BlockSpec((d2, tm // d2, w), lambda i: (0, i, 0))),
        scratch_shapes=[pltpu.VMEM((w // LANE, tm, LANE), F32)],
        compiler_params=pltpu.CompilerParams(dimension_semantics=("parallel",)),
        name="deinterleave",
    )(p, p)


def _gdn_out(of_ref, ob_ref, z_ref, hn_ref):
    o = of_ref[...].astype(F32) + ob_ref[...].astype(F32)
    z = z_ref[...].astype(F32)
    ys = []
    for hd in range(A_HEADS):
        sl = slice(hd * HEAD_DIM, (hd + 1) * HEAD_DIM)
        zh = z[:, sl]
        ys.append((_rms(o[:, sl], hn_ref[...]) * (zh * _sigmoid(zh))).astype(BF16))
    return jnp.concatenate(ys, axis=1)


def _short_conv(b_ref, bp_ref, bn_ref, cw_ref, *, tm, geom, halo):
    has_prev, has_next = _halo_flags(tm, geom)

    def gated(ref, rows):
        return ref[rows, B_WIDTH:2 * B_WIDTH].astype(F32) * ref[rows, 2 * B_WIDTH:].astype(F32)

    u = gated(b_ref, slice(None))
    prev_row = gated(bp_ref, slice(halo - 1, halo)) * has_prev
    next_row = gated(bn_ref, slice(0, 1)) * has_next
    return (b_ref[:, :B_WIDTH].astype(F32) * _conv3(u, prev_row, next_row, cw_ref)).astype(BF16)


def _group_mix(o_refs, l_refs, scratch, *, tm):
    (o0_ref, o1_ref, o2_ref), (l0_ref, l1_ref, l2_ref) = o_refs, l_refs
    so1, so2, sl1, sl2 = scratch
    for src_ref, dst_ref in ((o1_ref, so1), (o2_ref, so2), (l1_ref, sl1), (l2_ref, sl2)):
        dil = src_ref.shape[0]
        for hd in range(C_HPG):
            for r in range(dil):
                dst_ref[hd, pl.ds(r, tm // dil, stride=dil), :] = (
                    src_ref[r, :, hd * HEAD_DIM:(hd + 1) * HEAD_DIM].astype(F32))
    ys = []
    for hd in range(C_HPG):
        sl = slice(hd * HEAD_DIM, (hd + 1) * HEAD_DIM)
        l0, l1, l2 = l0_ref[:, sl], sl1[hd], sl2[hd]
        m = jnp.maximum(jnp.maximum(l0, l1), l2)
        e0, e1, e2 = jnp.exp(l0 - m), jnp.exp(l1 - m), jnp.exp(l2 - m)
        num = e0 * o0_ref[:, sl].astype(F32) + e1 * so1[hd] + e2 * so2[hd]
        ys.append((num / (e0 + e1 + e2)).astype(BF16))
    return jnp.concatenate(ys, axis=1)


def _merge_kernel(*refs, bounds, tm, geom, halo):
    nsrc = len(bounds) - 1
    h_refs = refs[:nsrc]
    (of_ref, ob_ref, z_ref, hn_ref, b_ref, bp_ref, bn_ref, cw_ref,
     o0_ref, o1_ref, o2_ref, l0_ref, l1_ref, l2_ref,
     g_ref, wa_ref, wb_ref, wc_ref, wo_ref, o_ref, so1, so2, sl1, sl2) = refs[nsrc:]
    d = D_MODEL
    y_c = _group_mix((o0_ref, o1_ref, o2_ref), (l0_ref, l1_ref, l2_ref), (so1, so2, sl1, sl2), tm=tm)
    merged = _sigmoid(g_ref[:, 2 * d:].astype(F32)) * _dot(y_c, wc_ref[...])
    y_b = _short_conv(b_ref, bp_ref, bn_ref, cw_ref, tm=tm, geom=geom, halo=halo)
    merged += _sigmoid(g_ref[:, d:2 * d].astype(F32)) * _dot(y_b, wb_ref[...])
    y_a = _gdn_out(of_ref, ob_ref, z_ref, hn_ref)
    merged += _sigmoid(g_ref[:, 0:d].astype(F32)) * _dot(y_a, wa_ref[...])
    delta = _dot(merged.astype(BF16), wo_ref[...])

    def residual(h_ref):
        o_ref[...] = h_ref[...] + delta

    _on_owner(h_refs, bounds, residual)


def _merge(p, hs, o_f, o_b, attn, head_norm, conv_b, wa, wb, wc, wo, layer, geom, *, tm=256, halo=16):
    t = p.shape[0]
    tm = min([tm] + [a.shape[0] for a in hs])
    d = D_MODEL
    nb = tm // halo
    last = t // halo - 1
    w3 = 3 * B_WIDTH
    h_specs, bounds = _row_specs(hs, tm, d)
    row = lambda w, col=0: pl.BlockSpec((tm, w), lambda i: (i, col))
    const = lambda shape: pl.BlockSpec((None,) + shape, lambda i: (layer, 0, 0), pipeline_mode=pl.Buffered(1))
    (o0, l0), (o1, l1), (o2, l2) = attn

    def res(a):
        dil = a.shape[0]
        if dil == 1:
            return pl.BlockSpec((None, tm, C_OUT), lambda i: (0, i, 0))
        return pl.BlockSpec((dil, tm // dil, C_OUT), lambda i: (0, i, 0))

    wbytes = (A_WIDTH + B_WIDTH + C_OUT + d) * d * 2
    vmem = (wbytes + 2 * tm * ((len(hs) + 1) * d * 4 + 3 * A_WIDTH * 2 + w3 * 2 + 3 * C_OUT * 6 + 3 * d * 2)
            + 4 * tm * C_OUT * 4 + 8 * tm * d * 4)
    return pl.pallas_call(
        functools.partial(_merge_kernel, bounds=bounds, tm=tm, geom=geom, halo=halo),
        out_shape=jax.ShapeDtypeStruct((t, d), F32),
        grid=(t // tm,),
        in_specs=h_specs + [
            row(A_WIDTH), row(A_WIDTH), row(A_WIDTH, P_Z // A_WIDTH),
            pl.BlockSpec((1, HEAD_DIM), lambda i: (0, 0)),
            row(w3, P_B3 // w3),
            pl.BlockSpec((halo, w3), lambda i: (jnp.maximum(i * nb - 1, 0), P_B3 // w3)),
            pl.BlockSpec((halo, w3), lambda i: (jnp.minimum((i + 1) * nb, last), P_B3 // w3)),
            pl.BlockSpec((3, B_WIDTH), lambda i: (0, 0)),
            res(o0), res(o1), res(o2), res(l0), res(l1), res(l2),
            row(3 * d, P_GATES // (3 * d)),
            const((A_WIDTH, d)), const((B_WIDTH, d)), const((C_OUT, d)), const((d, d)),
        ],
        out_specs=pl.BlockSpec((tm, d), lambda i: (i, 0)),
        scratch_shapes=[pltpu.VMEM((C_HPG, tm, HEAD_DIM), F32)] * 4,
        compiler_params=pltpu.CompilerParams(
            dimension_semantics=("parallel",), vmem_limit_bytes=_vmem_limit(vmem)),
        name="merge",
    )(*hs, o_f, o_b, p, head_norm.reshape(1, HEAD_DIM), p, p, p, conv_b, o0, o1, o2, l0, l1, l2,
      p, wa, wb, wc, wo)


def _mlp_kernel(h_ref, nw_ref, wu_ref, wd_ref, fw_ref, *refs, bounds, final):
    nout = len(bounds) - 1
    o_refs = refs[:nout]
    xn_ref, acc_ref = refs[nout:]
    f = pl.program_id(1)

    @pl.when(f == 0)
    def _():
        xn_ref[...] = _rms(h_ref[...], nw_ref[...]).astype(BF16)
        acc_ref[...] = jnp.zeros_like(acc_ref)

    up = jnp.maximum(_dot(xn_ref[...], wu_ref[...]), 0.0)
    acc_ref[...] += _dot((up * up).astype(BF16), wd_ref[...])

    def emit(o_ref):
        hn = h_ref[...] + acc_ref[...]
        o_ref[...] = _rms(hn, fw_ref[...]) if final else hn

    _on_owner(o_refs, bounds, emit, extra_cond=(f == pl.num_programs(1) - 1))


def _mlp(h, norm_w, wu, wd, final_w, layer, *, final, out_rows, tm=512, tf=1024):
    t = h.shape[0]
    tm = min([tm] + list(out_rows))
    d = D_MODEL
    outs = tuple(jax.ShapeDtypeStruct((r, d), F32) for r in out_rows)
    o_specs, bounds = _row_specs(outs, tm, d)
    kern = functools.partial(_mlp_kernel, bounds=bounds, final=final)
    vmem = (4 * tm * d * 4 + tm * d * 2 + tm * d * 4 + 4 * d * tf * 2 + 3 * tm * tf * 4
            + 2 * (len(outs) - 1) * tm * d * 4)
    return pl.pallas_call(
        kern,
        out_shape=outs,
        grid=(t // tm, D_FF // tf),
        in_specs=[
            pl.BlockSpec((tm, d), lambda i, f: (i, 0)),
            pl.BlockSpec((1, d), lambda i, f: (0, 0)),
            pl.BlockSpec((None, d, tf), lambda i, f: (layer, 0, f)),
            pl.BlockSpec((None, tf, d), lambda i, f: (layer, f, 0)),
            pl.BlockSpec((1, d), lambda i, f: (0, 0)),
        ],
        out_specs=tuple(o_specs),
        scratch_shapes=[pltpu.VMEM((tm, d), BF16), pltpu.VMEM((tm, d), F32)],
        compiler_params=pltpu.CompilerParams(
            dimension_semantics=("arbitrary", "arbitrary"), vmem_limit_bytes=_vmem_limit(vmem + (8 << 20))),
        name="mlp",
    )(h, norm_w.reshape(1, d), wu, wd, final_w.reshape(1, d))


PACK_ROWS = 512


def _w_in_block_sources():
    src = [P_QKV + PACK_ROWS * b for b in range(3072 // PACK_ROWS)]
    src += [4128 + PACK_ROWS * b for b in range(3072 // PACK_ROWS)]
    src += [3072 + PACK_ROWS * b for b in range(1024 // PACK_ROWS)]
    src += [4096]
    c0 = 7200
    src += [c0 + part * C_WIDTH + g * C_OUT for g in range(len(C_GROUPS)) for part in range(3)]
    src += [11808 + PACK_ROWS * b for b in range(3 * D_MODEL // PACK_ROWS)]
    assert len(src) == P_WIDTH // PACK_ROWS and P_AB == 14 * PACK_ROWS and P_CQKV == 15 * PACK_ROWS
    return np.asarray(src, np.int32)


def _cast_kernel(w_ref, o_ref):
    o_ref[...] = w_ref[...].astype(BF16)


def _pack_kernel(src_ref, w_ref, o_ref):
    del src_ref
    o_ref[...] = w_ref[0].astype(BF16)


def _pack_w_in(w_t):
    nl, cols, d = w_t.shape
    src = _w_in_block_sources()
    assert int(src.max()) + PACK_ROWS <= cols
    return pl.pallas_call(
        _pack_kernel,
        out_shape=jax.ShapeDtypeStruct((nl, P_WIDTH, d), BF16),
        grid_spec=pltpu.PrefetchScalarGridSpec(
            num_scalar_prefetch=1,
            grid=(nl, P_WIDTH // PACK_ROWS),
            in_specs=[pl.BlockSpec((pl.Element(1), pl.Element(PACK_ROWS), pl.Element(d)),
                                   lambda l, b, src_ref: (l, pl.multiple_of(src_ref[b], 8), 0))],
            out_specs=pl.BlockSpec((None, PACK_ROWS, d), lambda l, b, src_ref: (l, b, 0)),
        ),
        compiler_params=pltpu.CompilerParams(dimension_semantics=("parallel", "parallel")),
        name="pack_w_in",
    )(jnp.asarray(src), w_t)


def _cast_bf16(w, *, block_bytes=4 << 20):
    nl, r, c = w.shape
    tr = max(8, min(r, block_bytes // (c * 4)))
    assert r % tr == 0
    return pl.pallas_call(
        _cast_kernel,
        out_shape=jax.ShapeDtypeStruct(w.shape, BF16),
        grid=(nl, r // tr),
        in_specs=[pl.BlockSpec((None, tr, c), lambda l, i: (l, i, 0))],
        out_specs=pl.BlockSpec((None, tr, c), lambda l, i: (l, i, 0)),
        compiler_params=pltpu.CompilerParams(dimension_semantics=("parallel", "parallel")),
        name="cast_bf16",
    )(w)


def _trunk(xs, geom, rel_bias, norm_mix, w_in, conv_a, a_log, dt_bias, head_norm, conv_b,
           w_br_a, w_br_b, w_br_c, w_out, norm_mlp, w_up, w_down, norm_final):
    depth = w_in.shape[0]
    rows = tuple(a.shape[0] for a in xs)
    t = sum(rows)
    hs = list(xs)
    w_in_b = _pack_w_in(jnp.swapaxes(w_in, 1, 2))
    wa_b, wb_b, wc_b, wo_b, wu_b, wd_b = (_cast_bf16(w) for w in (w_br_a, w_br_b, w_br_c, w_out, w_up, w_down))
    for layer in range(depth):
        last = layer == depth - 1
        p, ab = _in_proj(hs, norm_mix[layer], w_in_b, layer)
        qkv, gb = _gdn_pre(p, ab, conv_a[layer], a_log[layer], dt_bias[layer], geom)
        gbt3 = gb[:, :32].reshape(t // GDN_CHUNK, GDN_CHUNK, 32).transpose(0, 2, 1)
        o_f, o_b = _gdn_scan(qkv, gb, gbt3, geom)
        r1, r2 = _deinterleave(p)
        attn = [_attn_group(p.reshape(1, t, P_WIDTH), P_CQKV // C_OUT, rel_bias, 0, geom),
                _attn_group(r1, 0, rel_bias, 1, geom),
                _attn_group(r2, 0, rel_bias, 2, geom)]
        h = _merge(p, hs, o_f, o_b, attn, head_norm[layer], conv_b[layer], wa_b, wb_b, wc_b, wo_b, layer, geom)
        hs = list(_mlp(h, norm_mlp[layer], wu_b, wd_b, norm_final, layer,
                       final=last, out_rows=rows if last else (t,)))
    return hs


def kernel(x_prompt, x_sample, rel_bias, norm_mix, w_in, conv_a, a_log, dt_bias, head_norm, conv_b,
           w_br_a, w_br_b, w_br_c, w_out, norm_mlp, w_up, w_down, norm_final):
    bp, sp, d = x_prompt.shape
    bs, ss, _ = x_sample.shape
    tp = bp * sp
    geom = (tp, sp, ss)
    xs = (x_prompt.reshape(tp, d), x_sample.reshape(bs * ss, d))
    y_p, y_s = _trunk(xs, geom, rel_bias, norm_mix, w_in, conv_a, a_log, dt_bias, head_norm, conv_b,
                      w_br_a, w_br_b, w_br_c, w_out, norm_mlp, w_up, w_down, norm_final)
    return y_p.reshape(bp, sp, d), y_s.reshape(bs, ss, d)
```

```python
import functools
import math

import jax
import jax.numpy as jnp
import numpy as np
from jax import lax
from jax.experimental import pallas as pl
from jax.experimental.pallas import tpu as pltpu

F32 = jnp.float32
BF16 = jnp.bfloat16

LANE = 128
VMEM_BYTES_V7X = 64 * 1024 * 1024

D_MODEL = 2048
A_HEADS = 8
HEAD_DIM = 128
A_WIDTH = A_HEADS * HEAD_DIM
GDN_CHUNK = 64
B_WIDTH = 1024
C_GROUPS = ((128, 1), (512, 4), (2048, 16))
C_HPG = 4
C_HEADS = 12
C_WIDTH = C_HEADS * HEAD_DIM
C_OUT = C_HPG * HEAD_DIM
N_BUCKETS = 32
REL_MAX_DIST = 2048
RADIUS = 64
D_FF = 4 * D_MODEL
EPS = 1e-6
NEG_INF = -1e30

P_QKV = 0
P_B3 = 3072
P_Z = 6144
P_AB = 7168
P_CQKV = 7680
P_GATES = 12288
P_WIDTH = 18432


def _vmem_limit(nbytes):
    return int(min(max(nbytes, 16 * 1024 * 1024), VMEM_BYTES_V7X - 8 * 1024 * 1024))


def _seq_bounds(t0, geom):
    tp, sp, ss = geom
    in_p = t0 < tp
    length = jnp.where(in_p, sp, ss)
    base = jnp.where(in_p, 0, tp)
    lo = base + ((t0 - base) // length) * length
    return lo, lo + length


def _rms(x, w):
    ms = jnp.mean(x * x, axis=-1, keepdims=True)
    return x * lax.rsqrt(ms + EPS) * w


def _sigmoid(x):
    return 1.0 / (1.0 + jnp.exp(-x))


def _dot(a, b):
    return jnp.dot(a, b, preferred_element_type=F32)


def _dot_nt(a, b):
    return lax.dot_general(a, b, (((1,), (1,)), ((), ())), preferred_element_type=F32)


def _dot_tn(a, b):
    return lax.dot_general(a, b, (((0,), (0,)), ((), ())), preferred_element_type=F32)


def _row_specs(srcs, tm, width, single_buffer=False):
    specs, bounds = [], [0]
    for a in srcs:
        n, off = a.shape[0] // tm, bounds[-1]
        assert a.shape[0] % tm == 0
        mode = dict(pipeline_mode=pl.Buffered(1)) if single_buffer and len(srcs) > 1 else {}
        specs.append(pl.BlockSpec((tm, width), lambda i, *_, off=off, n=n: (jnp.clip(i - off, 0, n - 1), 0), **mode))
        bounds.append(off + n)
    return specs, tuple(bounds)


def _on_owner(refs, bounds, fn, extra_cond=None):
    i = pl.program_id(0)
    for k, ref in enumerate(refs):
        cond = extra_cond
        if len(refs) > 1:
            own = (i >= bounds[k]) & (i < bounds[k + 1])
            cond = own if cond is None else cond & own
        if cond is None:
            fn(ref)
        else:
            pl.when(cond)(functools.partial(fn, ref))


def _inproj_kernel(*refs, bounds, ab_tile, ab_off):
    nsrc = len(bounds) - 1
    x_refs = refs[:nsrc]
    nw_ref, w_ref, p_ref, ab_ref, xn_ref = refs[nsrc:]
    j = pl.program_id(1)

    def norm(x_ref):
        xn_ref[...] = _rms(x_ref[...], nw_ref[...]).astype(BF16)

    _on_owner(x_refs, bounds, norm, extra_cond=(j == 0))
    acc = _dot_nt(xn_ref[...], w_ref[...])
    p_ref[...] = acc.astype(BF16)

    @pl.when(j == ab_tile)
    def _():
        ab_ref[...] = acc[:, ab_off:ab_off + LANE]


def _in_proj(hs, norm_w, w_packed, layer, *, tm=1024, tn=1024):
    t = sum(a.shape[0] for a in hs)
    tm = min([tm] + [a.shape[0] for a in hs])
    assert P_WIDTH % tn == 0
    x_specs, bounds = _row_specs(hs, tm, D_MODEL, single_buffer=True)
    kern = functools.partial(_inproj_kernel, bounds=bounds, ab_tile=P_AB // tn, ab_off=P_AB % tn)
    vmem = 2 * tm * D_MODEL * 4 + tm * D_MODEL * 2 + 2 * D_MODEL * tn * 2 + 2 * tm * tn * 2 + 2 * tm * tn * 4
    return pl.pallas_call(
        kern,
        out_shape=(jax.ShapeDtypeStruct((t, P_WIDTH), BF16), jax.ShapeDtypeStruct((t, LANE), F32)),
        grid=(t // tm, P_WIDTH // tn),
        in_specs=x_specs + [
            pl.BlockSpec((1, D_MODEL), lambda i, j: (0, 0)),
            pl.BlockSpec((None, tn, D_MODEL), lambda i, j: (layer, j, 0)),
        ],
        out_specs=(
            pl.BlockSpec((tm, tn), lambda i, j: (i, j)),
            pl.BlockSpec((tm, LANE), lambda i, j: (i, 0)),
        ),
        scratch_shapes=[pltpu.VMEM((tm, D_MODEL), BF16)],
        compiler_params=pltpu.CompilerParams(
            dimension_semantics=("parallel", "arbitrary"), vmem_limit_bytes=_vmem_limit(vmem + (8 << 20))),
        name="in_proj",
    )(*hs, norm_w.reshape(1, D_MODEL), w_packed)


def _conv3(x, prev_row, next_row, w_ref):
    tm = x.shape[0]
    rows = lax.broadcasted_iota(jnp.int32, (tm, 1), 0)
    x_m1 = jnp.where(rows == 0, prev_row, pltpu.roll(x, 1, 0))
    x_p1 = jnp.where(rows == tm - 1, next_row, pltpu.roll(x, tm - 1, 0))
    return x_m1 * w_ref[0:1, :] + x * w_ref[1:2, :] + x_p1 * w_ref[2:3, :]


def _halo_flags(tm, geom):
    t0 = pl.program_id(0) * tm
    lo, hi = _seq_bounds(t0, geom)
    return (t0 > lo).astype(F32), (t0 + tm < hi).astype(F32)


def _deinterleave_rows(x_ref, o_ref, scr_ref, *, tm):
    dil = o_ref.shape[0]
    for cb in range(x_ref.shape[1] // LANE):
        cols = slice(cb * LANE, (cb + 1) * LANE)
        scr_ref[cb] = x_ref[:, cols].astype(F32)
        for r in range(dil):
            o_ref[r, :, cols] = scr_ref[cb, pl.ds(r, tm // dil, stride=dil), :].astype(o_ref.dtype)


def _gdn_pre_kernel(x_ref, xp_ref, xn_ref, cw_ref, ab_ref, nalog_ref, dtb_ref, c1_ref, c2_ref,
                    o_ref, gb_ref, r1_ref, r2_ref, scr_ref, *, tm, geom, halo):
    _deinterleave_rows(c1_ref, r1_ref, scr_ref, tm=tm)
    _deinterleave_rows(c2_ref, r2_ref, scr_ref, tm=tm)

    has_prev, has_next = _halo_flags(tm, geom)
    x = x_ref[...].astype(F32)
    prev_row = xp_ref[halo - 1:halo, :].astype(F32) * has_prev
    next_row = xn_ref[0:1, :].astype(F32) * has_next
    y = _conv3(x, prev_row, next_row, cw_ref)
    y = y * _sigmoid(y)
    for hd in range(2 * A_HEADS):
        sl = slice(hd * HEAD_DIM, (hd + 1) * HEAD_DIM)
        yh = y[:, sl]
        inv = lax.rsqrt(jnp.sum(yh * yh, axis=-1, keepdims=True) + EPS)
        if hd < A_HEADS:
            inv = inv * (HEAD_DIM ** -0.5)
        o_ref[:, sl] = (yh * inv).astype(BF16)
    o_ref[:, 2 * A_WIDTH:] = y[:, 2 * A_WIDTH:].astype(BF16)

    ab = ab_ref[...]
    xs = ab + dtb_ref[...]
    softplus = jnp.maximum(xs, 0.0) + jnp.log(1.0 + jnp.exp(-jnp.abs(xs)))
    g = nalog_ref[...] * softplus
    lane = lax.broadcasted_iota(jnp.int32, ab.shape, 1)
    gb_ref[...] = jnp.where(lane < 2 * A_HEADS, g, _sigmoid(ab))


def _gdn_pre(p, ab, conv_w, a_log, dt_bias, geom, *, tm=256, halo=16):
    t = p.shape[0]
    tm = min(tm, t)
    w = 3 * A_WIDTH
    nb = tm // halo
    last = t // halo - 1
    nalog = jnp.zeros((1, LANE), F32).at[0, :2 * A_HEADS].set(-jnp.exp(a_log.reshape(-1)))
    dtb = jnp.zeros((1, LANE), F32).at[0, :2 * A_HEADS].set(dt_bias.reshape(-1))
    kern = functools.partial(_gdn_pre_kernel, tm=tm, geom=geom, halo=halo)
    wc = 3 * C_OUT
    d1, d2 = C_GROUPS[1][1], C_GROUPS[2][1]
    vmem = 2 * (tm * w * 2 * 2 + 2 * halo * w * 2) + 10 * tm * w * 4 + 8 * tm * wc * 2 + tm * wc * 4
    return pl.pallas_call(
        kern,
        out_shape=(jax.ShapeDtypeStruct((t, w), BF16), jax.ShapeDtypeStruct((t, LANE), F32),
                   jax.ShapeDtypeStruct((d1, t // d1, wc), BF16), jax.ShapeDtypeStruct((d2, t // d2, wc), BF16)),
        grid=(t // tm,),
        in_specs=[
            pl.BlockSpec((tm, w), lambda i: (i, P_QKV // w)),
            pl.BlockSpec((halo, w), lambda i: (jnp.maximum(i * nb - 1, 0), P_QKV // w)),
            pl.BlockSpec((halo, w), lambda i: (jnp.minimum((i + 1) * nb, last), P_QKV // w)),
            pl.BlockSpec((3, w), lambda i: (0, 0)),
            pl.BlockSpec((tm, LANE), lambda i: (i, 0)),
            pl.BlockSpec((1, LANE), lambda i: (0, 0)),
            pl.BlockSpec((1, LANE), lambda i: (0, 0)),
            pl.BlockSpec((tm, wc), lambda i: (i, P_CQKV // wc + 1)),
            pl.BlockSpec((tm, wc), lambda i: (i, P_CQKV // wc + 2)),
        ],
        out_specs=(
            pl.BlockSpec((tm, w), lambda i: (i, 0)),
            pl.BlockSpec((tm, LANE), lambda i: (i, 0)),
            pl.BlockSpec((d1, tm // d1, wc), lambda i: (0, i, 0)),
            pl.BlockSpec((d2, tm // d2, wc), lambda i: (0, i, 0)),
        ),
        scratch_shapes=[pltpu.VMEM((wc // LANE, tm, LANE), F32)],
        compiler_params=pltpu.CompilerParams(
            dimension_semantics=("parallel",), vmem_limit_bytes=_vmem_limit(vmem)),
        name="gdn_pre",
    )(p, p, p, conv_w, ab, nalog, dtb, p, p)


def _split3(x):
    x1 = x.astype(BF16)
    r1 = x - x1.astype(F32)
    x2 = r1.astype(BF16)
    r2 = r1 - x2.astype(F32)
    return x1, x2, r2.astype(BF16)


def _gdn_scan_kernel(qf_ref, kf_ref, vf_ref, gf_ref, grf_ref, qb_ref, kb_ref, vb_ref, gbk_ref, grb_ref,
                     of_ref, ob_ref, s_ref, wq_s, u_s, lhs2_s, egl_s, *, rb, nblk, geom):
    c = GDN_CHUNK
    nc = rb // c
    i = pl.program_id(0)
    t0f = i * rb
    t0b = (nblk - 1 - i) * rb
    lo_f, _ = _seq_bounds(t0f, geom)
    _, hi_b = _seq_bounds(t0b, geom)

    @pl.when(t0f == lo_f)
    def _():
        s_ref[0:A_HEADS] = jnp.zeros((A_HEADS, HEAD_DIM, HEAD_DIM), F32)

    @pl.when(t0b + rb == hi_b)
    def _():
        s_ref[A_HEADS:2 * A_HEADS] = jnp.zeros((A_HEADS, HEAD_DIM, HEAD_DIM), F32)

    ri = lax.broadcasted_iota(jnp.int32, (c, c), 0)
    ci = lax.broadcasted_iota(jnp.int32, (c, c), 1)
    lower_incl = ri >= ci
    upper_incl = ri <= ci
    tri_lo = jnp.where(lower_incl, 1.0, 0.0).astype(BF16)
    tri_up = jnp.where(upper_incl, 1.0, 0.0).astype(BF16)
    bd16 = (ri // 16) == (ci // 16)
    off32 = ((ri // 32) == (ci // 32)) & jnp.logical_not(bd16)
    off64 = (ri // 32) != (ci // 32)
    eye = jnp.where(ri == ci, 1.0, 0.0).astype(F32)
    bf = lambda x: x.astype(BF16)

    sides = (
        (0, qf_ref, kf_ref, vf_ref, gf_ref, grf_ref, tri_lo, tri_up, lower_incl, ri > ci),
        (1, qb_ref, kb_ref, vb_ref, gbk_ref, grb_ref, tri_up, tri_lo, upper_incl, ri < ci),
    )

    def prep(jp, carry):
        chains = []
        for direction, q_ref, k_ref, v_ref, g_ref, gr_ref, m_col, m_row, incl, strict in sides:
            for cidx in (2 * jp, 2 * jp + 1):
                r0 = pl.multiple_of(cidx * c, c)
                gcol = g_ref[pl.ds(r0, c), :]
                grow = gr_ref[cidx]
                g1, g2, g3 = _split3(gcol)
                gc_col = _dot(m_col, g1) + _dot(m_col, g2) + _dot(m_col, g3)
                h1, h2, h3 = _split3(grow)
                gc_row = _dot(h1, m_row) + _dot(h2, m_row) + _dot(h3, m_row)
                for hh in range(A_HEADS):
                    ch = direction * A_HEADS + hh
                    sl = slice(hh * HEAD_DIM, (hh + 1) * HEAD_DIM)
                    gcr = gc_row[ch:ch + 1, :]
                    chains.append(dict(
                        ch=ch, cidx=cidx, incl=incl, strict=strict,
                        q=q_ref[pl.ds(r0, c), sl], k=k_ref[pl.ds(r0, c), sl], v=v_ref[pl.ds(r0, c), sl],
                        gcr=gcr, beta_row=grow[2 * A_HEADS + ch:2 * A_HEADS + ch + 1, :],
                        gcc_b=jnp.broadcast_to(gc_col[:, ch:ch + 1], (c, HEAD_DIM)),
                        g_last=gcr[:, c - 1:c] if direction == 0 else gcr[:, 0:1]))

        gram = [_dot_nt(jnp.concatenate([x["q"], x["k"]], axis=0), x["k"]) for x in chains]
        zs, e32, e64 = [], [], []
        for x, gm in zip(chains, gram):
            decay = jnp.exp(jnp.where(x["incl"], x["gcc_b"][:, :c] - x["gcr"], NEG_INF))
            dec_beta = decay * x["beta_row"]
            k_t = jnp.transpose(x["k"].astype(F32))
            k_dec_t = k_t * (jnp.exp(x["g_last"] - x["gcr"]) * x["beta_row"])
            lhs2_s[x["cidx"], x["ch"]] = jnp.concatenate([bf(gm[:c] * dec_beta), bf(k_dec_t)], axis=0)
            egl_s[x["cidx"], x["ch"]] = jnp.broadcast_to(jnp.exp(x["g_last"]), (1, HEAD_DIM))
            l_mat = jnp.where(x["strict"], gm[c:] * dec_beta, 0.0)
            zs.append(jnp.where(bd16, -l_mat, 0.0))
            e32.append(bf(jnp.where(off32, l_mat, 0.0)))
            e64.append(bf(jnp.where(off64, l_mat, 0.0)))

        zb = [bf(z) for z in zs]
        xs = [eye + z for z in zs]
        pw = [bf(_dot(b, b)) for b in zb]
        for step in range(3):
            nxt = [bf(_dot(p, p)) for p in pw] if step < 2 else None
            xs = [x + _dot(bf(x), p) for x, p in zip(xs, pw)]
            pw = nxt
        for e in (e32, e64):
            xb = [bf(x) for x in xs]
            ts = [bf(_dot(b, a)) for b, a in zip(xb, e)]
            xs = [x - _dot(t, b) for x, t, b in zip(xs, ts, xb)]

        for x, t_inv in zip(chains, xs):
            eg = jnp.exp(x["gcc_b"])
            rhs = jnp.concatenate([x["v"], bf(x["k"].astype(F32) * eg)], axis=1)
            uw = _dot(bf(t_inv), rhs)
            u_s[x["cidx"], x["ch"]] = uw[:, :HEAD_DIM]
            wq_s[x["cidx"], x["ch"]] = jnp.concatenate(
                [bf(uw[:, HEAD_DIM:]), bf(x["q"].astype(F32) * eg)], axis=0)
        return carry

    lax.fori_loop(0, nc // 2, prep, 0)

    def scan(j, carry):
        chains = [(d * A_HEADS + hh, cidx, o_ref, slice(hh * HEAD_DIM, (hh + 1) * HEAD_DIM))
                  for d, cidx, o_ref in ((0, j, of_ref), (1, nc - 1 - j, ob_ref)) for hh in range(A_HEADS)]
        states = [s_ref[ch] for ch, _, _, _ in chains]
        ws_qs = [_dot(wq_s[cidx, ch], bf(s)) for (ch, cidx, _, _), s in zip(chains, states)]
        v_new = [bf(u_s[cidx, ch] - r[:c]) for (ch, cidx, _, _), r in zip(chains, ws_qs)]
        upd = [_dot(lhs2_s[cidx, ch], vn) for (ch, cidx, _, _), vn in zip(chains, v_new)]
        for (ch, cidx, o_ref, sl), s, r, r2 in zip(chains, states, ws_qs, upd):
            r0 = pl.multiple_of(cidx * c, c)
            o_ref[pl.ds(r0, c), sl] = (r[c:] + r2[:c]).astype(o_ref.dtype)
            s_ref[ch] = s * egl_s[cidx, ch] + r2[c:]
        return carry

    lax.fori_loop(0, nc, scan, 0)


def _gdn_scan(qkv, gb, gbt3, geom, *, rb=512):
    t = qkv.shape[0]
    rb = min(rb, t)
    nblk = t // rb
    ncb = rb // GDN_CHUNK
    kern = functools.partial(_gdn_scan_kernel, rb=rb, nblk=nblk, geom=geom)

    def fwd(col):
        return lambda i: (i, col)

    def bwd(col):
        return lambda i: (nblk - 1 - i, col)

    in_specs = []
    for mk in (fwd, bwd):
        in_specs += [
            pl.BlockSpec((rb, A_WIDTH), mk(0)),
            pl.BlockSpec((rb, A_WIDTH), mk(1)),
            pl.BlockSpec((rb, A_WIDTH), mk(2)),
            pl.BlockSpec((rb, LANE), mk(0)),
            pl.BlockSpec((ncb, 32, GDN_CHUNK), (lambda i: (i, 0, 0)) if mk is fwd else (lambda i: (nblk - 1 - i, 0, 0))),
        ]
    nch = 2 * A_HEADS
    assert ncb % 2 == 0
    vmem = (2 * (6 * rb * A_WIDTH * 2 + 2 * rb * LANE * 4 + 2 * rb * A_WIDTH * 2) + nch * HEAD_DIM * HEAD_DIM * 4
            + ncb * nch * HEAD_DIM * (2 * GDN_CHUNK * 2 + GDN_CHUNK * 4 + (GDN_CHUNK + HEAD_DIM) * 2 + 8 * 4))
    return pl.pallas_call(
        kern,
        out_shape=(jax.ShapeDtypeStruct((t, A_WIDTH), BF16), jax.ShapeDtypeStruct((t, A_WIDTH), BF16)),
        grid=(nblk,),
        in_specs=in_specs,
        out_specs=(pl.BlockSpec((rb, A_WIDTH), fwd(0)), pl.BlockSpec((rb, A_WIDTH), bwd(0))),
        scratch_shapes=[
            pltpu.VMEM((nch, HEAD_DIM, HEAD_DIM), F32),
            pltpu.VMEM((ncb, nch, 2 * GDN_CHUNK, HEAD_DIM), BF16),
            pltpu.VMEM((ncb, nch, GDN_CHUNK, HEAD_DIM), F32),
            pltpu.VMEM((ncb, nch, GDN_CHUNK + HEAD_DIM, GDN_CHUNK), BF16),
            pltpu.VMEM((ncb, nch, 1, HEAD_DIM), F32),
        ],
        compiler_params=pltpu.CompilerParams(
            dimension_semantics=("arbitrary",), vmem_limit_bytes=_vmem_limit(vmem + (16 << 20))),
        name="gdn_scan",
    )(qkv, qkv, qkv, gb, gbt3, qkv, qkv, qkv, gb, gbt3)


def _t5_bucket(rel):
    half = N_BUCKETS // 2
    exact = half // 2
    ret = np.where(rel > 0, half, 0)
    n = np.abs(rel)
    large = exact + (np.log(np.maximum(n, 1) / exact) / math.log(REL_MAX_DIST / exact) * (half - exact)).astype(np.int32)
    large = np.minimum(large, half - 1)
    return (ret + np.where(n < exact, n, large)).astype(np.int32)


def _attn_kernel(tab_ref, bkt_ref, p_ref, c_ref, n_ref, o_ref, lse_ref, bias_ref, *, tq, dil, geom):
    r = pl.program_id(0)
    qb = pl.program_id(1)
    sq = RADIUS
    nk = 3 * sq

    @pl.when((r == 0) & (qb == 0))
    def _():
        bkt = bkt_ref[...]
        for hh in range(C_HPG):
            acc = jnp.full((sq, nk), NEG_INF, F32)
            for b in range(N_BUCKETS):
                acc = jnp.where(bkt == b, tab_ref[b * C_HPG + hh], acc)
            bias_ref[hh] = acc

    lo, hi = _seq_bounds(qb * tq * dil, geom)
    scale = HEAD_DIM ** -0.5
    nsb = tq // sq

    def rows(part, ri, hh, s):
        cols = slice(part * C_OUT + hh * HEAD_DIM, part * C_OUT + (hh + 1) * HEAD_DIM)
        if s < 0:
            return p_ref[ri, :, cols]
        if s >= nsb:
            return n_ref[ri, :, cols]
        return c_ref[ri, s * sq:(s + 1) * sq, cols]

    def window(part, ri, hh, s):
        return jnp.concatenate([rows(part, ri, hh, s + ds) for ds in (-1, 0, 1)], axis=0)

    probs = [(ri, s, hh) for ri in range(c_ref.shape[0]) for s in range(nsb) for hh in range(C_HPG)]
    scores = [_dot_nt(rows(0, ri, hh, s), window(1, ri, hh, s)) for ri, s, hh in probs]
    ps, dens = [], []
    for (ri, s, hh), sc in zip(probs, scores):
        krow = qb * tq + (s - 1) * sq + lax.broadcasted_iota(jnp.int32, (1, nk), 1)
        in_seq = (krow * dil >= lo) & (krow * dil < hi)
        sc = jnp.where(in_seq, sc * scale + bias_ref[hh], NEG_INF)
        m = jnp.max(sc, axis=-1, keepdims=True)
        p = jnp.exp(sc - m)
        den = jnp.sum(p, axis=-1, keepdims=True)
        lse_ref[ri, s * sq:(s + 1) * sq, hh * HEAD_DIM:(hh + 1) * HEAD_DIM] = jnp.broadcast_to(
            m + jnp.log(den), (sq, HEAD_DIM))
        ps.append(p.astype(BF16))
        dens.append(den)
    for (ri, s, hh), p, den in zip(probs, ps, dens):
        o = _dot(p, window(2, ri, hh, s)) / den
        o_ref[ri, s * sq:(s + 1) * sq, hh * HEAD_DIM:(hh + 1) * HEAD_DIM] = o.astype(o_ref.dtype)


def _attn_group(src, col0, rel_bias, gi, geom, *, tq=512):
    dil, rows, _ = src.shape
    assert dil == C_GROUPS[gi][1] and C_GROUPS[gi][0] // (2 * dil) == RADIUS
    tq = min(tq, min(geom[1], geom[2]) // dil)
    assert tq >= RADIUS and rows % tq == 0 and (geom[1] // dil) % tq == 0 and (geom[2] // dil) % tq == 0
    nq = rows // tq
    sq, nk = RADIUS, 3 * RADIUS
    rel = np.arange(nk)[None, :] - RADIUS - np.arange(sq)[:, None]
    bkt = np.where(np.abs(rel) <= RADIUS, _t5_bucket(rel * dil), -1).astype(np.int32)
    tab = rel_bias[:, gi * C_HPG:(gi + 1) * C_HPG].astype(F32).reshape(-1)
    cw = C_OUT
    assert col0 % 3 == 0
    qkv_col = col0 // 3
    hpq = tq // RADIUS
    rbk = max(1, min(dil, 32 // (C_HPG * tq // sq)))
    assert dil % rbk == 0
    kern = functools.partial(_attn_kernel, tq=tq, dil=dil, geom=geom)
    return pl.pallas_call(
        kern,
        out_shape=(jax.ShapeDtypeStruct((dil, rows, cw), BF16), jax.ShapeDtypeStruct((dil, rows, cw), F32)),
        grid=(dil // rbk, nq),
        in_specs=[
            pl.BlockSpec(memory_space=pltpu.SMEM),
            pl.BlockSpec((sq, nk), lambda r, qb: (0, 0)),
            pl.BlockSpec((rbk, RADIUS, 3 * cw), lambda r, qb: (r, jnp.maximum(qb * hpq - 1, 0), qkv_col)),
            pl.BlockSpec((rbk, tq, 3 * cw), lambda r, qb: (r, qb, qkv_col)),
            pl.BlockSpec((rbk, RADIUS, 3 * cw),
                         lambda r, qb: (r, jnp.minimum((qb + 1) * hpq, rows // RADIUS - 1), qkv_col)),
        ],
        out_specs=(pl.BlockSpec((rbk, tq, cw), lambda r, qb: (r, qb, 0)),
                   pl.BlockSpec((rbk, tq, cw), lambda r, qb: (r, qb, 0))),
        scratch_shapes=[pltpu.VMEM((C_HPG, sq, nk), F32)],
        compiler_params=pltpu.CompilerParams(
            dimension_semantics=("arbitrary", "arbitrary"),
            vmem_limit_bytes=_vmem_limit(rbk * (2 * (tq + 2 * RADIUS) * 3 * cw * 2 + 2 * tq * cw * 6
                                                + C_HPG * tq * nk * 4 * 4) + (8 << 20))),
        name=f"attn_g{gi}",
    )(tab, jnp.asarray(bkt), src, src, src)


def _gdn_out(of_ref, ob_ref, z_ref, hn_ref):
    o = of_ref[...].astype(F32) + ob_ref[...].astype(F32)
    z = z_ref[...].astype(F32)
    ys = []
    for hd in range(A_HEADS):
        sl = slice(hd * HEAD_DIM, (hd + 1) * HEAD_DIM)
        zh = z[:, sl]
        ys.append((_rms(o[:, sl], hn_ref[...]) * (zh * _sigmoid(zh))).astype(BF16))
    return jnp.concatenate(ys, axis=1)


def _short_conv(b_ref, bp_ref, bn_ref, cw_ref, *, tm, geom, halo):
    has_prev, has_next = _halo_flags(tm, geom)

    def gated(ref, rows):
        return ref[rows, B_WIDTH:2 * B_WIDTH].astype(F32) * ref[rows, 2 * B_WIDTH:].astype(F32)

    u = gated(b_ref, slice(None))
    prev_row = gated(bp_ref, slice(halo - 1, halo)) * has_prev
    next_row = gated(bn_ref, slice(0, 1)) * has_next
    return (b_ref[:, :B_WIDTH].astype(F32) * _conv3(u, prev_row, next_row, cw_ref)).astype(BF16)


def _group_mix(o_refs, l_refs, scratch, *, tm):
    (o0_ref, o1_ref, o2_ref), (l0_ref, l1_ref, l2_ref) = o_refs, l_refs
    so1, so2, sl1, sl2 = scratch
    for src_ref, dst_ref in ((o1_ref, so1), (o2_ref, so2), (l1_ref, sl1), (l2_ref, sl2)):
        dil = src_ref.shape[0]
        for hd in range(C_HPG):
            for r in range(dil):
                dst_ref[hd, pl.ds(r, tm // dil, stride=dil), :] = (
                    src_ref[r, :, hd * HEAD_DIM:(hd + 1) * HEAD_DIM].astype(F32))
    ys = []
    for hd in range(C_HPG):
        sl = slice(hd * HEAD_DIM, (hd + 1) * HEAD_DIM)
        l0, l1, l2 = l0_ref[:, sl], sl1[hd], sl2[hd]
        m = jnp.maximum(jnp.maximum(l0, l1), l2)
        e0, e1, e2 = jnp.exp(l0 - m), jnp.exp(l1 - m), jnp.exp(l2 - m)
        num = e0 * o0_ref[:, sl].astype(F32) + e1 * so1[hd] + e2 * so2[hd]
        ys.append((num / (e0 + e1 + e2)).astype(BF16))
    return jnp.concatenate(ys, axis=1)


def _merge_kernel(*refs, bounds, tm, geom, halo):
    nsrc = len(bounds) - 1
    h_refs = refs[:nsrc]
    (of_ref, ob_ref, z_ref, hn_ref, b_ref, bp_ref, bn_ref, cw_ref,
     o0_ref, o1_ref, o2_ref, l0_ref, l1_ref, l2_ref,
     g_ref, wa_ref, wb_ref, wc_ref, wo_ref, o_ref, so1, so2, sl1, sl2) = refs[nsrc:]
    d = D_MODEL
    y_c = _group_mix((o0_ref, o1_ref, o2_ref), (l0_ref, l1_ref, l2_ref), (so1, so2, sl1, sl2), tm=tm)
    merged = _sigmoid(g_ref[:, 2 * d:].astype(F32)) * _dot(y_c, wc_ref[...])
    y_b = _short_conv(b_ref, bp_ref, bn_ref, cw_ref, tm=tm, geom=geom, halo=halo)
    merged += _sigmoid(g_ref[:, d:2 * d].astype(F32)) * _dot(y_b, wb_ref[...])
    y_a = _gdn_out(of_ref, ob_ref, z_ref, hn_ref)
    merged += _sigmoid(g_ref[:, 0:d].astype(F32)) * _dot(y_a, wa_ref[...])
    delta = _dot(merged.astype(BF16), wo_ref[...])

    def residual(h_ref):
        o_ref[...] = h_ref[...] + delta

    _on_owner(h_refs, bounds, residual)


def _merge(p, hs, o_f, o_b, attn, head_norm, conv_b, wa, wb, wc, wo, layer, geom, *, tm=256, halo=16):
    t = p.shape[0]
    tm = min([tm] + [a.shape[0] for a in hs])
    d = D_MODEL
    nb = tm // halo
    last = t // halo - 1
    w3 = 3 * B_WIDTH
    h_specs, bounds = _row_specs(hs, tm, d)
    row = lambda w, col=0: pl.BlockSpec((tm, w), lambda i: (i, col))
    const = lambda shape: pl.BlockSpec((None,) + shape, lambda i: (layer, 0, 0), pipeline_mode=pl.Buffered(1))
    (o0, l0), (o1, l1), (o2, l2) = attn

    def res(a):
        dil = a.shape[0]
        if dil == 1:
            return pl.BlockSpec((None, tm, C_OUT), lambda i: (0, i, 0))
        return pl.BlockSpec((dil, tm // dil, C_OUT), lambda i: (0, i, 0))

    wbytes = (A_WIDTH + B_WIDTH + C_OUT + d) * d * 2
    vmem = (wbytes + 2 * tm * ((len(hs) + 1) * d * 4 + 3 * A_WIDTH * 2 + w3 * 2 + 3 * C_OUT * 6 + 3 * d * 2)
            + 4 * tm * C_OUT * 4 + 8 * tm * d * 4)
    return pl.pallas_call(
        functools.partial(_merge_kernel, bounds=bounds, tm=tm, geom=geom, halo=halo),
        out_shape=jax.ShapeDtypeStruct((t, d), F32),
        grid=(t // tm,),
        in_specs=h_specs + [
            row(A_WIDTH), row(A_WIDTH), row(A_WIDTH, P_Z // A_WIDTH),
            pl.BlockSpec((1, HEAD_DIM), lambda i: (0, 0)),
            row(w3, P_B3 // w3),
            pl.BlockSpec((halo, w3), lambda i: (jnp.maximum(i * nb - 1, 0), P_B3 // w3)),
            pl.BlockSpec((halo, w3), lambda i: (jnp.minimum((i + 1) * nb, last), P_B3 // w3)),
            pl.BlockSpec((3, B_WIDTH), lambda i: (0, 0)),
            res(o0), res(o1), res(o2), res(l0), res(l1), res(l2),
            row(3 * d, P_GATES // (3 * d)),
            const((A_WIDTH, d)), const((B_WIDTH, d)), const((C_OUT, d)), const((d, d)),
        ],
        out_specs=pl.BlockSpec((tm, d), lambda i: (i, 0)),
        scratch_shapes=[pltpu.VMEM((C_HPG, tm, HEAD_DIM), F32)] * 4,
        compiler_params=pltpu.CompilerParams(
            dimension_semantics=("parallel",), vmem_limit_bytes=_vmem_limit(vmem)),
        name="merge",
    )(*hs, o_f, o_b, p, head_norm.reshape(1, HEAD_DIM), p, p, p, conv_b, o0, o1, o2, l0, l1, l2,
      p, wa, wb, wc, wo)


def _mlp_kernel(h_ref, nw_ref, wu_ref, wd_ref, fw_ref, *refs, bounds, final):
    nout = len(bounds) - 1
    o_refs = refs[:nout]
    xn_ref, acc_ref = refs[nout:]
    f = pl.program_id(1)

    @pl.when(f == 0)
    def _():
        xn_ref[...] = _rms(h_ref[...], nw_ref[...]).astype(BF16)
        acc_ref[...] = jnp.zeros_like(acc_ref)

    up = jnp.maximum(_dot(xn_ref[...], wu_ref[...]), 0.0)
    acc_ref[...] += _dot((up * up).astype(BF16), wd_ref[...])

    def emit(o_ref):
        hn = h_ref[...] + acc_ref[...]
        o_ref[...] = _rms(hn, fw_ref[...]) if final else hn

    _on_owner(o_refs, bounds, emit, extra_cond=(f == pl.num_programs(1) - 1))


def _mlp(h, norm_w, wu, wd, final_w, layer, *, final, out_rows, tm=512, tf=1024):
    t = h.shape[0]
    tm = min([tm] + list(out_rows))
    d = D_MODEL
    outs = tuple(jax.ShapeDtypeStruct((r, d), F32) for r in out_rows)
    o_specs, bounds = _row_specs(outs, tm, d)
    kern = functools.partial(_mlp_kernel, bounds=bounds, final=final)
    vmem = (4 * tm * d * 4 + tm * d * 2 + tm * d * 4 + 4 * d * tf * 2 + 3 * tm * tf * 4
            + 2 * (len(outs) - 1) * tm * d * 4)
    return pl.pallas_call(
        kern,
        out_shape=outs,
        grid=(t // tm, D_FF // tf),
        in_specs=[
            pl.BlockSpec((tm, d), lambda i, f: (i, 0)),
            pl.BlockSpec((1, d), lambda i, f: (0, 0)),
            pl.BlockSpec((None, d, tf), lambda i, f: (layer, 0, f)),
            pl.BlockSpec((None, tf, d), lambda i, f: (layer, f, 0)),
            pl.BlockSpec((1, d), lambda i, f: (0, 0)),
        ],
        out_specs=tuple(o_specs),
        scratch_shapes=[pltpu.VMEM((tm, d), BF16), pltpu.VMEM((tm, d), F32)],
        compiler_params=pltpu.CompilerParams(
            dimension_semantics=("arbitrary", "arbitrary"), vmem_limit_bytes=_vmem_limit(vmem + (8 << 20))),
        name="mlp",
    )(h, norm_w.reshape(1, d), wu, wd, final_w.reshape(1, d))


PACK_ROWS = 512


def _w_in_block_sources():
    splits = (3 * A_WIDTH, A_WIDTH, 2 * A_HEADS, 2 * A_HEADS, B_WIDTH, B_WIDTH, B_WIDTH,
              C_WIDTH, C_WIDTH, C_WIDTH, 3 * D_MODEL)
    off = np.concatenate([[0], np.cumsum(splits)])
    a_qkv, a_z, logits, b3, c_q, gates = (int(off[i]) for i in (0, 1, 2, 4, 7, 10))
    blocks = lambda start, width: [start + PACK_ROWS * b for b in range(width // PACK_ROWS)]
    src = blocks(a_qkv, 3 * A_WIDTH) + blocks(b3, 3 * B_WIDTH) + blocks(a_z, A_WIDTH) + [logits]
    src += [c_q + part * C_WIDTH + g * C_OUT for g in range(len(C_GROUPS)) for part in range(3)]
    src += blocks(gates, 3 * D_MODEL)
    assert len(src) == P_WIDTH // PACK_ROWS
    assert (src[P_B3 // PACK_ROWS], src[P_Z // PACK_ROWS], src[P_AB // PACK_ROWS]) == (b3, a_z, logits)
    assert (src[P_CQKV // PACK_ROWS], src[P_GATES // PACK_ROWS]) == (c_q, gates)
    return np.asarray(src, np.int32)


def _cast_kernel(w_ref, o_ref):
    o_ref[...] = w_ref[...].astype(BF16)


def _pack_kernel(src_ref, w_ref, o_ref):
    del src_ref
    o_ref[...] = w_ref[0].astype(BF16)


def _pack_w_in(w_t):
    nl, cols, d = w_t.shape
    src = _w_in_block_sources()
    assert int(src.max()) + PACK_ROWS <= cols
    return pl.pallas_call(
        _pack_kernel,
        out_shape=jax.ShapeDtypeStruct((nl, P_WIDTH, d), BF16),
        grid_spec=pltpu.PrefetchScalarGridSpec(
            num_scalar_prefetch=1,
            grid=(nl, P_WIDTH // PACK_ROWS),
            in_specs=[pl.BlockSpec((pl.Element(1), pl.Element(PACK_ROWS), pl.Element(d)),
                                   lambda l, b, src_ref: (l, pl.multiple_of(src_ref[b], 8), 0))],
            out_specs=pl.BlockSpec((None, PACK_ROWS, d), lambda l, b, src_ref: (l, b, 0)),
        ),
        compiler_params=pltpu.CompilerParams(dimension_semantics=("parallel", "parallel")),
        name="pack_w_in",
    )(jnp.asarray(src), w_t)


def _cast_bf16(w, *, block_bytes=4 << 20):
    nl, r, c = w.shape
    tr = max(8, min(r, block_bytes // (c * 4)))
    assert r % tr == 0
    return pl.pallas_call(
        _cast_kernel,
        out_shape=jax.ShapeDtypeStruct(w.shape, BF16),
        grid=(nl, r // tr),
        in_specs=[pl.BlockSpec((None, tr, c), lambda l, i: (l, i, 0))],
        out_specs=pl.BlockSpec((None, tr, c), lambda l, i: (l, i, 0)),
        compiler_params=pltpu.CompilerParams(dimension_semantics=("parallel", "parallel")),
        name="cast_bf16",
    )(w)


def _trunk(xs, geom, rel_bias, norm_mix, w_in, conv_a, a_log, dt_bias, head_norm, conv_b,
           w_br_a, w_br_b, w_br_c, w_out, norm_mlp, w_up, w_down, norm_final):
    depth = w_in.shape[0]
    rows = tuple(a.shape[0] for a in xs)
    t = sum(rows)
    hs = list(xs)
    w_in_b = _pack_w_in(jnp.swapaxes(w_in, 1, 2))
    wa_b, wb_b, wc_b, wo_b, wu_b, wd_b = (_cast_bf16(w) for w in (w_br_a, w_br_b, w_br_c, w_out, w_up, w_down))
    for layer in range(depth):
        last = layer == depth - 1
        p, ab = _in_proj(hs, norm_mix[layer], w_in_b, layer)
        qkv, gb, r1, r2 = _gdn_pre(p, ab, conv_a[layer], a_log[layer], dt_bias[layer], geom)
        gbt3 = gb[:, :32].reshape(t // GDN_CHUNK, GDN_CHUNK, 32).transpose(0, 2, 1)
        o_f, o_b = _gdn_scan(qkv, gb, gbt3, geom)
        attn = [_attn_group(p.reshape(1, t, P_WIDTH), P_CQKV // C_OUT, rel_bias, 0, geom),
                _attn_group(r1, 0, rel_bias, 1, geom),
                _attn_group(r2, 0, rel_bias, 2, geom)]
        h = _merge(p, hs, o_f, o_b, attn, head_norm[layer], conv_b[layer], wa_b, wb_b, wc_b, wo_b, layer, geom)
        hs = list(_mlp(h, norm_mlp[layer], wu_b, wd_b, norm_final, layer,
                       final=last, out_rows=rows if last else (t,)))
    return hs


def kernel(x_prompt, x_sample, rel_bias, norm_mix, w_in, conv_a, a_log, dt_bias, head_norm, conv_b,
           w_br_a, w_br_b, w_br_c, w_out, norm_mlp, w_up, w_down, norm_final):
    bp, sp, d = x_prompt.shape
    bs, ss, _ = x_sample.shape
    tp = bp * sp
    geom = (tp, sp, ss)
    xs = (x_prompt.reshape(tp, d), x_sample.reshape(bs * ss, d))
    y_p, y_s = _trunk(xs, geom, rel_bias, norm_mix, w_in, conv_a, a_log, dt_bias, head_norm, conv_b,
                      w_br_a, w_br_b, w_br_c, w_out, norm_mlp, w_up, w_down, norm_final)
    return y_p.reshape(bp, sp, d), y_s.reshape(bs, ss, d)
```
